```python
import jax, jax.numpy as jnp
from jax import lax
import numpy as np

D_MODEL = 1024
BATCH = 8
SEQ = 4096
DEPTH = 1

GRID_W = 64
CTX_LEN = 256
N_Q_HEADS = 8
N_KV_HEADS = 2
HEAD_DIM = 64
GROUP = N_Q_HEADS // N_KV_HEADS
ATTN_DIM = N_Q_HEADS * HEAD_DIM
KV_DIM = N_KV_HEADS * HEAD_DIM
CONV_DIM = D_MODEL - ATTN_DIM
MIX_DIM = ATTN_DIM + CONV_DIM
CONV_K = 3
OFF_K = ATTN_DIM
OFF_V = OFF_K + KV_DIM
OFF_B = OFF_V + KV_DIM
OFF_C = OFF_B + CONV_DIM
OFF_U = OFF_C + CONV_DIM
MIX_IN_DIM = OFF_U + CONV_DIM
D_FF = 2816
N_MOD = 9
ROPE_THETA = 10000.0
Q_BLOCK = 128
ATTN_SCALE = HEAD_DIM ** -0.5
EPS = 1e-6

kernel_name = "hybrid_dit_attn_shortconv_macaron"


def rmsnorm(x, g):
    x32 = x.astype(jnp.float32)
    y = x32 * lax.rsqrt(jnp.mean(x32 * x32, axis=-1, keepdims=True) + EPS)
    return (y * g.astype(jnp.float32)).astype(x.dtype)


def modulate(h, shift, scale):
    return h * (1 + scale) + shift


def swiglu(h, w_in, w_out):
    gate, up = jnp.split(h @ w_in, 2, axis=-1)
    return (jax.nn.silu(gate) * up) @ w_out


def axial_rope_tables(n_tokens):
    rows = n_tokens // GRID_W
    row = jnp.repeat(jnp.arange(rows, dtype=jnp.float32), GRID_W)
    col = jnp.tile(jnp.arange(GRID_W, dtype=jnp.float32), rows)
    n_freq = HEAD_DIM // 4
    inv = ROPE_THETA ** (-jnp.arange(n_freq, dtype=jnp.float32) / n_freq)
    ang = jnp.concatenate([row[:, None] * inv, col[:, None] * inv], axis=-1)
    return jnp.cos(ang)[:, None, :], jnp.sin(ang)[:, None, :]


def apply_rope(x, cos, sin):
    xr = x.astype(jnp.float32).reshape(x.shape[:-1] + (HEAD_DIM // 2, 2))
    x0, x1 = xr[..., 0], xr[..., 1]
    out = jnp.stack([x0 * cos - x1 * sin, x0 * sin + x1 * cos], axis=-1)
    return out.reshape(x.shape).astype(x.dtype)


def gqa_softmax(q, k, v):
    s = jnp.einsum('bqkgd,bskd->bkgqs', q, k, preferred_element_type=jnp.float32) * ATTN_SCALE
    p = jax.nn.softmax(s, axis=-1).astype(v.dtype)
    return jnp.einsum('bkgqs,bskd->bqkgd', p, v)


def latent_attention(q, k, v, kc, vc):
    b, s = q.shape[:2]
    nb = s // Q_BLOCK
    k_all = jnp.concatenate([kc, k], axis=1)
    v_all = jnp.concatenate([vc, v], axis=1)
    qb = q.reshape(b, nb, Q_BLOCK, N_KV_HEADS, GROUP, HEAD_DIM).transpose(1, 0, 2, 3, 4, 5)
    o = lax.map(lambda qi: gqa_softmax(qi, k_all, v_all), qb)
    return o.transpose(1, 0, 2, 3, 4, 5).reshape(b, s, ATTN_DIM)


def context_attention(q, kc, vc):
    b, l = q.shape[:2]
    o = gqa_softmax(q.reshape(b, l, N_KV_HEADS, GROUP, HEAD_DIM), kc, vc)
    return o.reshape(b, l, ATTN_DIM)


def short_conv3(u, w):
    up = jnp.pad(u, ((0, 0), (1, 1), (0, 0)))
    return up[:, :-2] * w[0] + up[:, 1:-1] * w[1] + up[:, 2:] * w[2]


def split_heads(p, q_norm, k_norm):
    b, l = p.shape[:2]
    q = rmsnorm(p[..., :OFF_K].reshape(b, l, N_Q_HEADS, HEAD_DIM), q_norm)
    k = rmsnorm(p[..., OFF_K:OFF_V].reshape(b, l, N_KV_HEADS, HEAD_DIM), k_norm)
    v = p[..., OFF_V:OFF_B].reshape(b, l, N_KV_HEADS, HEAD_DIM)
    gb, gc, u = p[..., OFF_B:OFF_C], p[..., OFF_C:OFF_U], p[..., OFF_U:]
    return q, k, v, gb, gc, u


def setup_inputs(seed: int = 0) -> dict:
    key = jax.random.key(seed)
    ks = jax.random.split(key, 24)
    f32 = jnp.float32
    nrm = lambda k, shape, s: jax.random.normal(k, shape, f32) * s
    gain = lambda k, shape: 1.0 + 0.05 * jax.random.normal(k, shape, f32)
    return {
        "x": nrm(ks[0], (BATCH, SEQ, D_MODEL), 1.0),
        "c": nrm(ks[1], (BATCH, D_MODEL), 1.0),
        "ctx": nrm(ks[2], (BATCH, CTX_LEN, D_MODEL), 1.0),
        "c_ctx": nrm(ks[3], (D_MODEL,), 1.0),
        "w_ada": nrm(ks[4], (DEPTH, D_MODEL, N_MOD * D_MODEL), 0.5 * D_MODEL ** -0.5),
        "b_ada": nrm(ks[5], (DEPTH, N_MOD * D_MODEL), 0.02),
        "norm_ffn1": gain(ks[6], (DEPTH, D_MODEL)),
        "w_ffn1_in": nrm(ks[7], (DEPTH, D_MODEL, 2 * D_FF), D_MODEL ** -0.5),
        "w_ffn1_out": nrm(ks[8], (DEPTH, D_FF, D_MODEL), D_FF ** -0.5),
        "norm_mix": gain(ks[9], (DEPTH, D_MODEL)),
        "w_mix_in": nrm(ks[10], (DEPTH, D_MODEL, MIX_IN_DIM), D_MODEL ** -0.5),
        "q_norm": gain(ks[11], (DEPTH, HEAD_DIM)),
        "k_norm": gain(ks[12], (DEPTH, HEAD_DIM)),
        "conv_w": nrm(ks[13], (DEPTH, CONV_K, CONV_DIM), CONV_K ** -0.5),
        "attn_out_norm": gain(ks[14], (DEPTH, ATTN_DIM)),
        "conv_out_norm": gain(ks[15], (DEPTH, CONV_DIM)),
        "w_mix_out": nrm(ks[16], (DEPTH, MIX_DIM, D_MODEL), MIX_DIM ** -0.5),
        "norm_ffn2": gain(ks[17], (DEPTH, D_MODEL)),
        "w_ffn2_in": nrm(ks[18], (DEPTH, D_MODEL, 2 * D_FF), D_MODEL ** -0.5),
        "w_ffn2_out": nrm(ks[19], (DEPTH, D_FF, D_MODEL), D_FF ** -0.5),
        "final_norm": gain(ks[20], (D_MODEL,)),
    }


def reference(x, c, ctx, c_ctx, w_ada, b_ada, norm_ffn1, w_ffn1_in, w_ffn1_out,
              norm_mix, w_mix_in, q_norm, k_norm, conv_w, attn_out_norm, conv_out_norm,
              w_mix_out, norm_ffn2, w_ffn2_in, w_ffn2_out, final_norm):
    b, s, d = x.shape
    cos, sin = axial_rope_tables(s)

    for l in range(DEPTH):
        update_ctx = l < DEPTH - 1
        m = (jax.nn.silu(c) @ w_ada[l] + b_ada[l]).reshape(b, 1, N_MOD, d)
        mc = (jax.nn.silu(c_ctx) @ w_ada[l] + b_ada[l]).reshape(1, 1, N_MOD, d)

        x = x + 0.5 * m[:, :, 2] * swiglu(modulate(rmsnorm(x, norm_ffn1[l]), m[:, :, 0], m[:, :, 1]),
                                          w_ffn1_in[l], w_ffn1_out[l])
        ctx = ctx + 0.5 * mc[:, :, 2] * swiglu(modulate(rmsnorm(ctx, norm_ffn1[l]), mc[:, :, 0], mc[:, :, 1]),
                                               w_ffn1_in[l], w_ffn1_out[l])

        px = modulate(rmsnorm(x, norm_mix[l]), m[:, :, 3], m[:, :, 4]) @ w_mix_in[l]
        pc = modulate(rmsnorm(ctx, norm_mix[l]), mc[:, :, 3], mc[:, :, 4]) @ w_mix_in[l]
        q, k, v, gb, gc, u = split_heads(px, q_norm[l], k_norm[l])
        qc, kc, vc, gbc, gcc, uc = split_heads(pc, q_norm[l], k_norm[l])

        a = latent_attention(apply_rope(q, cos, sin), apply_rope(k, cos, sin), v, kc, vc)
        sc = gb * short_conv3(gc * u, conv_w[l])
        mix = jnp.concatenate([rmsnorm(a, attn_out_norm[l]), rmsnorm(sc, conv_out_norm[l])], axis=-1)
        x = x + m[:, :, 5] * (mix @ w_mix_out[l])

        if update_ctx:
            ac = context_attention(qc, kc, vc)
            scc = gbc * short_conv3(gcc * uc, conv_w[l])
            mixc = jnp.concatenate([rmsnorm(ac, attn_out_norm[l]), rmsnorm(scc, conv_out_norm[l])], axis=-1)
            ctx = ctx + mc[:, :, 5] * (mixc @ w_mix_out[l])
            ctx = ctx + 0.5 * mc[:, :, 8] * swiglu(modulate(rmsnorm(ctx, norm_ffn2[l]), mc[:, :, 6], mc[:, :, 7]),
                                                   w_ffn2_in[l], w_ffn2_out[l])

        x = x + 0.5 * m[:, :, 8] * swiglu(modulate(rmsnorm(x, norm_ffn2[l]), m[:, :, 6], m[:, :, 7]),
                                          w_ffn2_in[l], w_ffn2_out[l])

    return rmsnorm(x, final_norm)
```

```python
import functools
import math

import jax
import jax.numpy as jnp
from jax.experimental import pallas as pl
from jax.experimental.pallas import tpu as pltpu

F32 = jnp.float32
BF16 = jnp.bfloat16

EPS = 1e-6
ROPE_THETA = 10000.0
GRID_W = 64
HEAD_DIM = 64
N_Q_HEADS = 8
N_KV_HEADS = 2
ATTN_DIM = N_Q_HEADS * HEAD_DIM
KV_DIM = N_KV_HEADS * HEAD_DIM
N_MOD = 9
LANES = 128
FF_CHUNK = 256
VMEM_LIMIT = 56 * 1024 * 1024
LOG2E = math.log2(math.e)


def _dot(a, b):
    return jnp.dot(a, b, preferred_element_type=F32)


def _dot_nt(a, b):
    return jax.lax.dot_general(a, b, (((1,), (1,)), ((), ())), preferred_element_type=F32)


def _rms(x, gain_row):
    ms = jnp.mean(x * x, axis=-1, keepdims=True)
    return x * jax.lax.rsqrt(ms + EPS) * gain_row


def _norm_mod(x, gain_row, mod, base):
    shift = mod[base:base + 1, :]
    scale = mod[base + 1:base + 2, :]
    return _rms(x, gain_row) * (1.0 + scale) + shift


def _swiglu(h, win_ref, wout_ref, act_ref):
    d_ff = wout_ref.shape[0]
    for c in range(d_ff // FF_CHUNK):
        lo = c * FF_CHUNK
        g = _dot(h, win_ref[:, lo:lo + FF_CHUNK])
        u = _dot(h, win_ref[:, d_ff + lo:d_ff + lo + FF_CHUNK])
        act_ref[:, lo:lo + FF_CHUNK] = (g * jax.nn.sigmoid(g) * u).astype(BF16)
    return _dot(act_ref[...], wout_ref[...])


def _ada_kernel(c_ref, w_ref, b_ref, o_ref):
    c = c_ref[...]
    h = (c * jax.nn.sigmoid(c)).astype(BF16)
    o_ref[...] = _dot(h, w_ref[...].astype(BF16)) + b_ref[...]


def _ada_mod(cc, w_ada, b_ada, tn=1024):
    rows, d = cc.shape
    n = w_ada.shape[1]
    return pl.pallas_call(
        _ada_kernel,
        grid=(n // tn,),
        in_specs=[pl.BlockSpec((rows, d), lambda j: (0, 0)),
                  pl.BlockSpec((d, tn), lambda j: (0, j)),
                  pl.BlockSpec((1, tn), lambda j: (0, j))],
        out_specs=pl.BlockSpec((rows, tn), lambda j: (0, j)),
        out_shape=jax.ShapeDtypeStruct((rows, n), F32),
        compiler_params=pltpu.CompilerParams(dimension_semantics=("arbitrary",),
                                             vmem_limit_bytes=VMEM_LIMIT),
        name="ada_mod",
    )(cc, w_ada, b_ada)


def _ffn_kernel(x_ref, mod_ref, g_ref, win_ref, wout_ref, o_ref, act_ref, *, base):
    x = x_ref[0]
    mod = mod_ref[0]
    h = _norm_mod(x, g_ref[...], mod, base).astype(BF16)
    y = _swiglu(h, win_ref, wout_ref, act_ref)
    o_ref[0] = x + 0.5 * mod[base + 2:base + 3, :] * y


def _resident(shape):
    return pl.BlockSpec(shape, lambda *_: (0,) * len(shape), pipeline_mode=pl.Buffered(1))


def _ffn(x, mods, gain, w_in, w_out, *, base, tm):
    b, s, d = x.shape
    d_ff = w_out.shape[0]
    return pl.pallas_call(
        functools.partial(_ffn_kernel, base=base),
        grid=(b, s // tm),
        in_specs=[pl.BlockSpec((1, tm, d), lambda i, j: (i, j, 0)),
                  pl.BlockSpec((1, N_MOD, d), lambda i, j: (i, 0, 0)),
                  _resident((1, d)),
                  _resident(w_in.shape),
                  _resident(w_out.shape)],
        out_specs=pl.BlockSpec((1, tm, d), lambda i, j: (i, j, 0)),
        out_shape=jax.ShapeDtypeStruct(x.shape, F32),
        scratch_shapes=[pltpu.VMEM((tm, d_ff), BF16)],
        compiler_params=pltpu.CompilerParams(dimension_semantics=("arbitrary", "arbitrary"),
                                             vmem_limit_bytes=VMEM_LIMIT),
        name="ffn",
    )(x, mods, gain, w_in, w_out)


def _pair_swap(x, even_lane):
    n = x.shape[-1]
    return jnp.where(even_lane, pltpu.roll(x, n - 1, 1), pltpu.roll(x, 1, 1))


def _head_sumsq(x, ones_ref):
    width = min(x.shape[-1], ones_ref.shape[0])
    ones = ones_ref[:width, :width]
    parts = []
    for j in range(x.shape[-1] // width):
        xj = x[:, j * width:(j + 1) * width]
        sq = xj * xj
        hi = sq.astype(BF16)
        lo = (sq - hi.astype(F32)).astype(BF16)
        parts.append(_dot(hi, ones) + _dot(lo, ones))
    return parts[0] if len(parts) == 1 else jnp.concatenate(parts, axis=-1)


def _qk_heads(p, gain_row, ones_ref):
    ss = _head_sumsq(p, ones_ref)
    return p * jax.lax.rsqrt(ss * (1.0 / HEAD_DIM) + EPS) * gain_row


def _mix_in_kernel(*refs, base, latent):
    if latent:
        (x_ref, mod_ref, g_ref, w_ref, qg_ref, kg_ref, ones_ref, cos_ref, sin_ref,
         q_ref, k_ref, v_ref, gb_ref, cu_ref) = refs
    else:
        x_ref, mod_ref, g_ref, w_ref, kg_ref, ones_ref, k_ref, v_ref = refs
    x = x_ref[0]
    h = _norm_mod(x, g_ref[...], mod_ref[0], base).astype(BF16)
    lane = jax.lax.broadcasted_iota(jnp.int32, (x.shape[0], LANES), 1)
    low_half = lane < HEAD_DIM

    off_k, off_v, off_b = ATTN_DIM, ATTN_DIM + KV_DIM, ATTN_DIM + 2 * KV_DIM
    conv_dim = (w_ref.shape[1] - off_b) // 3

    k = _qk_heads(_dot(h, w_ref[:, off_k:off_v]), kg_ref[...], ones_ref)
    if latent:
        even = (lane & 1) == 0
        cos = cos_ref[...]
        sin = sin_ref[...]
        k = k * cos + _pair_swap(k, even) * sin
    kr = pltpu.roll(k, HEAD_DIM, 1)
    zero = jnp.zeros_like(k)
    k_ref[0, 0] = jnp.where(low_half, k, zero).astype(BF16)
    k_ref[0, 1] = jnp.where(low_half, zero, kr).astype(BF16)
    k_ref[0, 2] = jnp.where(low_half, kr, zero).astype(BF16)
    k_ref[0, 3] = jnp.where(low_half, zero, k).astype(BF16)

    v = _dot(h, w_ref[:, off_v:off_b])
    vr = pltpu.roll(v, HEAD_DIM, 1)
    ones_col = jnp.where(lane == HEAD_DIM, 1.0, 0.0)
    v_ref[0, 0] = jnp.where(low_half, v, ones_col).astype(BF16)
    v_ref[0, 1] = jnp.where(low_half, vr, ones_col).astype(BF16)

    if latent:
        q = _qk_heads(_dot(h, w_ref[:, :off_k]), qg_ref[...], ones_ref)
        for j in range(ATTN_DIM // LANES):
            qj = q[:, j * LANES:(j + 1) * LANES]
            q_ref[0, j] = (qj * cos + _pair_swap(qj, even) * sin).astype(BF16)
        gb_ref[0] = _dot(h, w_ref[:, off_b:off_b + conv_dim])
        gc = _dot(h, w_ref[:, off_b + conv_dim:off_b + 2 * conv_dim])
        u = _dot(h, w_ref[:, off_b + 2 * conv_dim:])
        cu_ref[0] = gc * u


def _mix_in(x, mods, gain, w, qg, kg, ones, cos, sin, *, base, tm, latent):
    b, s, d = x.shape
    conv_dim = (w.shape[1] - ATTN_DIM - 2 * KV_DIM) // 3
    tok = lambda i, j: (i, j, 0)
    head = lambda i, j: (i, 0, j, 0)
    in_specs = [pl.BlockSpec((1, tm, d), tok),
                pl.BlockSpec((1, N_MOD, d), lambda i, j: (i, 0, 0)),
                _resident((1, d)),
                _resident(w.shape)]
    args = [x, mods, gain, w]
    if latent:
        in_specs.append(_resident((1, ATTN_DIM)))
        args.append(qg)
    in_specs += [_resident((1, LANES)), _resident(ones.shape)]
    args += [kg, ones]
    out_specs = [pl.BlockSpec((1, 4, tm, LANES), head),
                 pl.BlockSpec((1, 2, tm, LANES), head)]
    out_shape = [jax.ShapeDtypeStruct((b, 4, s, LANES), BF16),
                 jax.ShapeDtypeStruct((b, 2, s, LANES), BF16)]
    if latent:
        in_specs += [pl.BlockSpec((tm, LANES), lambda i, j: (j, 0))] * 2
        args += [cos, sin]
        out_specs = [pl.BlockSpec((1, 4, tm, LANES), head)] + out_specs + [
            pl.BlockSpec((1, tm, conv_dim), tok)] * 2
        out_shape = [jax.ShapeDtypeStruct((b, 4, s, LANES), BF16)] + out_shape + [
            jax.ShapeDtypeStruct((b, s, conv_dim), F32)] * 2
    return pl.pallas_call(
        functools.partial(_mix_in_kernel, base=base, latent=latent),
        grid=(b, s // tm),
        in_specs=in_specs,
        out_specs=out_specs,
        out_shape=out_shape,
        compiler_params=pltpu.CompilerParams(dimension_semantics=("arbitrary", "arbitrary"),
                                             vmem_limit_bytes=VMEM_LIMIT),
        name="mix_in_latent" if latent else "mix_in_context",
    )(*args)


def _attn_kernel(q_ref, k_ref, v_ref, kc_ref, vc_ref, g_ref, o_ref, m_ref, acc_ref, *, tk):
    tq = q_ref.shape[2]
    n_chunks = k_ref.shape[2] // tk
    lane = jax.lax.broadcasted_iota(jnp.int32, (tq, LANES), 1)
    low_half = lane < HEAD_DIM

    m_ref[...] = jnp.full(m_ref.shape, -jnp.inf, F32)
    acc_ref[...] = jnp.zeros(acc_ref.shape, F32)

    def step(kv, k_a, k_b, v_e):
        q2 = q_ref[0, 2 * kv:2 * kv + 2].reshape(2 * tq, LANES)
        s = jnp.concatenate([_dot_nt(q2, k_a), _dot_nt(q2, k_b)], axis=0)
        m_old = m_ref[kv]
        m_new = jnp.maximum(m_old, jnp.max(s, axis=-1, keepdims=True))
        p = jnp.exp2(s - m_new).astype(BF16)
        acc_ref[kv] = acc_ref[kv] * jnp.exp2(m_old - m_new) + _dot(p, v_e)
        m_ref[kv] = m_new

    def body(c, carry):
        rows = pl.ds(pl.multiple_of(c * tk, tk), tk)
        for kv in range(N_KV_HEADS):
            step(kv, k_ref[0, 2 * kv, rows, :], k_ref[0, 2 * kv + 1, rows, :], v_ref[0, kv, rows, :])
        return carry

    jax.lax.fori_loop(0, n_chunks, body, 0)
    for kv in range(N_KV_HEADS):
        step(kv, kc_ref[0, 2 * kv], kc_ref[0, 2 * kv + 1], vc_ref[0, kv])

    outs = []
    for kv in range(N_KV_HEADS):
        acc = acc_ref[kv]
        o = acc / acc[:, HEAD_DIM:HEAD_DIM + 1]
        h0, h2, h1, h3 = (o[i * tq:(i + 1) * tq] for i in range(4))
        outs.append(jnp.where(low_half, h0, pltpu.roll(h1, HEAD_DIM, 1)))
        outs.append(jnp.where(low_half, h2, pltpu.roll(h3, HEAD_DIM, 1)))
    a = jnp.concatenate(outs, axis=-1)
    o_ref[0] = _rms(a, g_ref[...]).astype(BF16)


def _attention(q, k, v, kc, vc, gain, *, tq, tk):
    b, _, s, _ = q.shape
    lc = kc.shape[2]
    per_batch = lambda i, j: (i, 0, 0, 0)
    return pl.pallas_call(
        functools.partial(_attn_kernel, tk=tk),
        grid=(b, s // tq),
        in_specs=[pl.BlockSpec((1, 4, tq, LANES), lambda i, j: (i, 0, j, 0)),
                  pl.BlockSpec((1, 4, s, LANES), per_batch),
                  pl.BlockSpec((1, 2, s, LANES), per_batch),
                  pl.BlockSpec((1, 4, lc, LANES), per_batch),
                  pl.BlockSpec((1, 2, lc, LANES), per_batch),
                  _resident((1, ATTN_DIM))],
        out_specs=pl.BlockSpec((1, tq, ATTN_DIM), lambda i, j: (i, j, 0)),
        out_shape=jax.ShapeDtypeStruct((b, s, ATTN_DIM), BF16),
        scratch_shapes=[pltpu.VMEM((N_KV_HEADS, 4 * tq, 1), F32),
                        pltpu.VMEM((N_KV_HEADS, 4 * tq, LANES), F32)],
        compiler_params=pltpu.CompilerParams(dimension_semantics=("arbitrary", "arbitrary"),
                                             vmem_limit_bytes=VMEM_LIMIT),
        name="attention",
    )(q, k, v, kc, vc, gain)


def _mix_out_kernel(x_ref, an_ref, gb_ref, cu_ref, cup_ref, cun_ref, mod_ref, cw_ref, cg_ref,
                    wmix_ref, g2_ref, win_ref, wout_ref, fg_ref, o_ref, act_ref):
    j = pl.program_id(1)
    tm = x_ref.shape[1]
    x = x_ref[0]
    mod = mod_ref[0]
    cu = cu_ref[0]
    row = jax.lax.broadcasted_iota(jnp.int32, cu.shape, 0)
    halo = cup_ref.shape[1]
    prev_row = jnp.where(j == 0, 0.0, cup_ref[0, halo - 1:halo, :])
    next_row = jnp.where(j == pl.num_programs(1) - 1, 0.0, cun_ref[0, 0:1, :])
    cu_prev = jnp.where(row == 0, prev_row, pltpu.roll(cu, 1, 0))
    cu_next = jnp.where(row == tm - 1, next_row, pltpu.roll(cu, tm - 1, 0))
    cw = cw_ref[...]
    sc = gb_ref[0] * (cu_prev * cw[0:1, :] + cu * cw[1:2, :] + cu_next * cw[2:3, :])
    scn = _rms(sc, cg_ref[...]).astype(BF16)
    attn_dim = an_ref.shape[2]
    mix = _dot(an_ref[0], wmix_ref[:attn_dim, :]) + _dot(scn, wmix_ref[attn_dim:, :])
    x = x + mod[5:6, :] * mix
    h = _norm_mod(x, g2_ref[...], mod, 6).astype(BF16)
    y = _swiglu(h, win_ref, wout_ref, act_ref)
    x = x + 0.5 * mod[8:9, :] * y
    o_ref[0] = _rms(x, fg_ref[...])


def _mix_out(x, an, gb, cu, mods, conv_w, conv_gain, w_mix, gain2, w_in, w_out, final_gain, *, tm, halo=8):
    b, s, d = x.shape
    d_ff = w_out.shape[0]
    conv_dim = cu.shape[2]
    tok = lambda i, j: (i, j, 0)
    per = tm // halo
    n_halo = s // halo
    return pl.pallas_call(
        _mix_out_kernel,
        grid=(b, s // tm),
        in_specs=[pl.BlockSpec((1, tm, d), tok),
                  pl.BlockSpec((1, tm, an.shape[2]), tok),
                  pl.BlockSpec((1, tm, conv_dim), tok),
                  pl.BlockSpec((1, tm, conv_dim), tok),
                  pl.BlockSpec((1, halo, conv_dim), lambda i, j: (i, jnp.maximum(j * per - 1, 0), 0)),
                  pl.BlockSpec((1, halo, conv_dim), lambda i, j: (i, jnp.minimum((j + 1) * per, n_halo - 1), 0)),
                  pl.BlockSpec((1, N_MOD, d), lambda i, j: (i, 0, 0)),
                  _resident(conv_w.shape),
                  _resident((1, conv_dim)),
                  _resident(w_mix.shape),
                  _resident((1, d)),
                  _resident(w_in.shape),
                  _resident(w_out.shape),
                  _resident((1, d))],
        out_specs=pl.BlockSpec((1, tm, d), tok),
        out_shape=jax.ShapeDtypeStruct(x.shape, F32),
        scratch_shapes=[pltpu.VMEM((tm, d_ff), BF16)],
        compiler_params=pltpu.CompilerParams(dimension_semantics=("arbitrary", "arbitrary"),
                                             vmem_limit_bytes=VMEM_LIMIT),
        name="mix_out_ffn2",
    )(x, an, gb, cu, cu, cu, mods, conv_w, conv_gain, w_mix, gain2, w_in, w_out, final_gain)


def _rope_tables(n_tokens):
    rows = n_tokens // GRID_W
    row = jnp.repeat(jnp.arange(rows, dtype=F32), GRID_W)
    col = jnp.tile(jnp.arange(GRID_W, dtype=F32), rows)
    n_freq = HEAD_DIM // 4
    inv = ROPE_THETA ** (-jnp.arange(n_freq, dtype=F32) / n_freq)
    ang = jnp.concatenate([row[:, None] * inv, col[:, None] * inv], axis=-1)
    cos = jnp.repeat(jnp.cos(ang), 2, axis=-1)
    sign = jnp.tile(jnp.array([-1.0, 1.0], F32), HEAD_DIM // 2)
    sin = jnp.repeat(jnp.sin(ang), 2, axis=-1) * sign
    return jnp.tile(cos, (1, LANES // HEAD_DIM)), jnp.tile(sin, (1, LANES // HEAD_DIM))


def kernel(x, c, ctx, c_ctx, w_ada, b_ada, norm_ffn1, w_ffn1_in, w_ffn1_out, norm_mix, w_mix_in,
           q_norm, k_norm, conv_w, attn_out_norm, conv_out_norm, w_mix_out, norm_ffn2, w_ffn2_in,
           w_ffn2_out, final_norm):
    b, s, d = x.shape
    lc = ctx.shape[1]
    assert w_ada.shape[0] == 1, "single-layer problem"
    row = lambda v: v.reshape(1, -1)

    n_rows = 16
    cc = jnp.concatenate([c, c_ctx[None], jnp.zeros((n_rows - b - 1, d), F32)], axis=0)
    mall = _ada_mod(cc, w_ada[0], row(b_ada[0]))
    mods = mall[:b].reshape(b, N_MOD, d)
    mods_c = jnp.broadcast_to(mall[b].reshape(1, N_MOD, d), (b, N_MOD, d))

    w1_in, w1_out = w_ffn1_in[0].astype(BF16), w_ffn1_out[0].astype(BF16)
    w2_in, w2_out = w_ffn2_in[0].astype(BF16), w_ffn2_out[0].astype(BF16)
    w_in, w_mo = w_mix_in[0].astype(BF16), w_mix_out[0].astype(BF16)

    x1 = _ffn(x, mods, row(norm_ffn1[0]), w1_in, w1_out, base=0, tm=512)
    c1 = _ffn(ctx, mods_c, row(norm_ffn1[0]), w1_in, w1_out, base=0, tm=lc)

    cos, sin = _rope_tables(s)
    qg = jnp.tile(q_norm[0], N_Q_HEADS).reshape(1, ATTN_DIM) * (HEAD_DIM ** -0.5 * LOG2E)
    kg = jnp.tile(k_norm[0], N_KV_HEADS).reshape(1, KV_DIM)
    head_id = jnp.arange(2 * LANES) // HEAD_DIM
    ones = (head_id[:, None] == head_id[None, :]).astype(BF16)

    q, k, v, gb, cu = _mix_in(x1, mods, row(norm_mix[0]), w_in, qg, kg, ones, cos, sin,
                              base=3, tm=512, latent=True)
    kc, vc = _mix_in(c1, mods_c, row(norm_mix[0]), w_in, None, kg, ones, None, None,
                     base=3, tm=lc, latent=False)

    an = _attention(q, k, v, kc, vc, row(attn_out_norm[0]), tq=128, tk=512)

    return _mix_out(x1, an, gb, cu, mods, conv_w[0], row(conv_out_norm[0]), w_mo, row(norm_ffn2[0]),
                    w2_in, w2_out, row(final_norm), tm=512)
```

```python
import functools
import math

import jax
import jax.numpy as jnp
from jax.experimental import pallas as pl
from jax.experimental.pallas import tpu as pltpu

F32 = jnp.float32
BF16 = jnp.bfloat16

EPS = 1e-6
ROPE_THETA = 10000.0
GRID_W = 64
HEAD_DIM = 64
N_Q_HEADS = 8
N_KV_HEADS = 2
ATTN_DIM = N_Q_HEADS * HEAD_DIM
KV_DIM = N_KV_HEADS * HEAD_DIM
N_MOD = 9
LANES = 128
FF_CHUNK = 256
VMEM_LIMIT = 56 * 1024 * 1024
LOG2E = math.log2(math.e)
MAX_EXP2_ARG = 60.0


def _dot(a, b):
    return jnp.dot(a, b, preferred_element_type=F32)


def _dot_nt(a, b):
    return jax.lax.dot_general(a, b, (((1,), (1,)), ((), ())), preferred_element_type=F32)


def _rms(x, gain_row):
    ms = jnp.mean(x * x, axis=-1, keepdims=True)
    return x * jax.lax.rsqrt(ms + EPS) * gain_row


def _norm_mod(x, gain_row, mod, base):
    shift = mod[base:base + 1, :]
    scale = mod[base + 1:base + 2, :]
    return _rms(x, gain_row) * (1.0 + scale) + shift


def _swiglu(h, win_ref, wout_ref, act_ref):
    d_ff = wout_ref.shape[0]
    for c in range(d_ff // FF_CHUNK):
        lo = c * FF_CHUNK
        g = _dot(h, win_ref[:, lo:lo + FF_CHUNK])
        u = _dot(h, win_ref[:, d_ff + lo:d_ff + lo + FF_CHUNK])
        act_ref[:, lo:lo + FF_CHUNK] = (g * jax.nn.sigmoid(g) * u).astype(BF16)
    return _dot(act_ref[...], wout_ref[...])


def _ada_kernel(c_ref, w_ref, b_ref, o_ref):
    c = c_ref[...]
    h = (c * jax.nn.sigmoid(c)).astype(BF16)
    o_ref[...] = _dot(h, w_ref[...].astype(BF16)) + b_ref[...]


def _ada_mod(cc, w_ada, b_ada, tn=1024):
    rows, d = cc.shape
    n = w_ada.shape[1]
    return pl.pallas_call(
        _ada_kernel,
        grid=(n // tn,),
        in_specs=[pl.BlockSpec((rows, d), lambda j: (0, 0)),
                  pl.BlockSpec((d, tn), lambda j: (0, j)),
                  pl.BlockSpec((1, tn), lambda j: (0, j))],
        out_specs=pl.BlockSpec((rows, tn), lambda j: (0, j)),
        out_shape=jax.ShapeDtypeStruct((rows, n), F32),
        compiler_params=pltpu.CompilerParams(dimension_semantics=("arbitrary",),
                                             vmem_limit_bytes=VMEM_LIMIT),
        name="ada_mod",
    )(cc, w_ada, b_ada)


def _ffn_kernel(x_ref, mod_ref, g_ref, win_ref, wout_ref, o_ref, act_ref, *, base):
    x = x_ref[0]
    mod = mod_ref[0]
    h = _norm_mod(x, g_ref[...], mod, base).astype(BF16)
    y = _swiglu(h, win_ref, wout_ref, act_ref)
    o_ref[0] = x + 0.5 * mod[base + 2:base + 3, :] * y


def _resident(shape):
    return pl.BlockSpec(shape, lambda *_: (0,) * len(shape), pipeline_mode=pl.Buffered(1))


def _ffn(x, mods, gain, w_in, w_out, *, base, tm):
    b, s, d = x.shape
    d_ff = w_out.shape[0]
    return pl.pallas_call(
        functools.partial(_ffn_kernel, base=base),
        grid=(b, s // tm),
        in_specs=[pl.BlockSpec((1, tm, d), lambda i, j: (i, j, 0)),
                  pl.BlockSpec((1, N_MOD, d), lambda i, j: (i, 0, 0)),
                  _resident((1, d)),
                  _resident(w_in.shape),
                  _resident(w_out.shape)],
        out_specs=pl.BlockSpec((1, tm, d), lambda i, j: (i, j, 0)),
        out_shape=jax.ShapeDtypeStruct(x.shape, F32),
        scratch_shapes=[pltpu.VMEM((tm, d_ff), BF16)],
        compiler_params=pltpu.CompilerParams(dimension_semantics=("arbitrary", "arbitrary"),
                                             vmem_limit_bytes=VMEM_LIMIT),
        name="ffn",
    )(x, mods, gain, w_in, w_out)


def _pair_swap(x, even_lane):
    n = x.shape[-1]
    return jnp.where(even_lane, pltpu.roll(x, n - 1, 1), pltpu.roll(x, 1, 1))


def _head_sumsq(x, ones_ref):
    width = min(x.shape[-1], ones_ref.shape[0])
    ones = ones_ref[:width, :width]
    parts = []
    for j in range(x.shape[-1] // width):
        xj = x[:, j * width:(j + 1) * width]
        sq = xj * xj
        hi = sq.astype(BF16)
        lo = (sq - hi.astype(F32)).astype(BF16)
        parts.append(_dot(hi, ones) + _dot(lo, ones))
    return parts[0] if len(parts) == 1 else jnp.concatenate(parts, axis=-1)


def _qk_heads(p, gain_row, ones_ref):
    ss = _head_sumsq(p, ones_ref)
    return p * jax.lax.rsqrt(ss * (1.0 / HEAD_DIM) + EPS) * gain_row


def _mix_in_kernel(*refs, base, latent):
    if latent:
        (x_ref, mod_ref, g_ref, w_ref, qg_ref, kg_ref, ones_ref, cos_ref, sin_ref,
         q_ref, k_ref, v_ref, gb_ref, cu_ref) = refs
    else:
        x_ref, mod_ref, g_ref, w_ref, kg_ref, ones_ref, k_ref, v_ref = refs
    x = x_ref[0]
    h = _norm_mod(x, g_ref[...], mod_ref[0], base).astype(BF16)
    lane = jax.lax.broadcasted_iota(jnp.int32, (x.shape[0], LANES), 1)
    low_half = lane < HEAD_DIM

    off_k, off_v, off_b = ATTN_DIM, ATTN_DIM + KV_DIM, ATTN_DIM + 2 * KV_DIM
    conv_dim = (w_ref.shape[1] - off_b) // 3

    first = 0 if latent else off_k
    last = w_ref.shape[1] if latent else off_b
    proj = _dot(h, w_ref[:, first:last])
    seg = lambda lo, hi: proj[:, lo - first:hi - first]
    k = _qk_heads(seg(off_k, off_v), kg_ref[...], ones_ref)
    if latent:
        even = (lane & 1) == 0
        cos = cos_ref[...]
        sin = sin_ref[...]
        k = k * cos + _pair_swap(k, even) * sin
    kr = pltpu.roll(k, HEAD_DIM, 1)
    k_ref[0, 0] = jnp.where(low_half, k, kr).astype(BF16)
    k_ref[0, 1] = jnp.where(low_half, kr, k).astype(BF16)

    v = seg(off_v, off_b)
    vr = pltpu.roll(v, HEAD_DIM, 1)
    ones_col = jnp.where(lane == HEAD_DIM, 1.0, 0.0)
    v_ref[0, 0] = jnp.where(low_half, v, ones_col).astype(BF16)
    v_ref[0, 1] = jnp.where(low_half, vr, ones_col).astype(BF16)

    if latent:
        q = _qk_heads(seg(0, off_k), qg_ref[...], ones_ref)
        for j in range(ATTN_DIM // LANES):
            qj = q[:, j * LANES:(j + 1) * LANES]
            q_ref[0, j] = (qj * cos + _pair_swap(qj, even) * sin).astype(BF16)
        gb_ref[0] = seg(off_b, off_b + conv_dim)
        gc = seg(off_b + conv_dim, off_b + 2 * conv_dim)
        u = seg(off_b + 2 * conv_dim, off_b + 3 * conv_dim)
        cu_ref[0] = gc * u


def _mix_in(x, mods, gain, w, qg, kg, ones, cos, sin, *, base, tm, latent):
    b, s, d = x.shape
    conv_dim = (w.shape[1] - ATTN_DIM - 2 * KV_DIM) // 3
    tok = lambda i, j: (i, j, 0)
    head = lambda i, j: (i, 0, j, 0)
    in_specs = [pl.BlockSpec((1, tm, d), tok),
                pl.BlockSpec((1, N_MOD, d), lambda i, j: (i, 0, 0)),
                _resident((1, d)),
                _resident(w.shape)]
    args = [x, mods, gain, w]
    if latent:
        in_specs.append(_resident((1, ATTN_DIM)))
        args.append(qg)
    in_specs += [_resident((1, LANES)), _resident(ones.shape)]
    args += [kg, ones]
    out_specs = [pl.BlockSpec((1, N_KV_HEADS, tm, LANES), head)] * 2
    out_shape = [jax.ShapeDtypeStruct((b, N_KV_HEADS, s, LANES), BF16)] * 2
    if latent:
        in_specs += [pl.BlockSpec((tm, LANES), lambda i, j: (j, 0))] * 2
        args += [cos, sin]
        out_specs = [pl.BlockSpec((1, 4, tm, LANES), head)] + out_specs + [
            pl.BlockSpec((1, tm, conv_dim), tok)] * 2
        out_shape = [jax.ShapeDtypeStruct((b, 4, s, LANES), BF16)] + out_shape + [
            jax.ShapeDtypeStruct((b, s, conv_dim), F32)] * 2
    return pl.pallas_call(
        functools.partial(_mix_in_kernel, base=base, latent=latent),
        grid=(b, s // tm),
        in_specs=in_specs,
        out_specs=out_specs,
        out_shape=out_shape,
        compiler_params=pltpu.CompilerParams(dimension_semantics=("arbitrary", "arbitrary"),
                                             vmem_limit_bytes=VMEM_LIMIT),
        name="mix_in_latent" if latent else "mix_in_context",
    )(*args)


def _attn_kernel(q_ref, k_ref, v_ref, kc_ref, vc_ref, g_ref, o_ref, *, tk, stable):
    tq = q_ref.shape[2]
    n_chunks = k_ref.shape[2] // tk
    low2 = jax.lax.broadcasted_iota(jnp.int32, (2 * tq, LANES), 1) < HEAD_DIM
    low1 = jax.lax.broadcasted_iota(jnp.int32, (tq, LANES), 1) < HEAD_DIM

    pairs = []
    for kv in range(N_KV_HEADS):
        q2 = q_ref[0, 2 * kv:2 * kv + 2].reshape(2 * tq, LANES).astype(F32)
        q4 = jnp.concatenate([jnp.where(low2, q2, 0.0), jnp.where(low2, 0.0, q2)], axis=0).astype(BF16)
        chunks = [(k_ref[0, kv, c * tk:(c + 1) * tk, :], v_ref[0, kv, c * tk:(c + 1) * tk, :])
                  for c in range(n_chunks)]
        chunks.append((kc_ref[0, kv], vc_ref[0, kv]))
        m = acc = None
        for k_c, v_c in chunks:
            s = _dot_nt(q4, k_c)
            if stable:
                m_c = jnp.max(s, axis=-1, keepdims=True)
                m_new = m_c if m is None else jnp.maximum(m, m_c)
                pv = _dot(jnp.exp2(s - m_new).astype(BF16), v_c)
                acc = pv if acc is None else acc * jnp.exp2(m - m_new) + pv
                m = m_new
            else:
                pv = _dot(jnp.exp2(s).astype(BF16), v_c)
                acc = pv if acc is None else acc + pv
        o = acc / acc[:, HEAD_DIM:HEAD_DIM + 1]
        h0, h2, h1, h3 = (o[i * tq:(i + 1) * tq] for i in range(4))
        pairs.append(jnp.where(low1, h0, pltpu.roll(h1, HEAD_DIM, 1)))
        pairs.append(jnp.where(low1, h2, pltpu.roll(h3, HEAD_DIM, 1)))
    a = jnp.concatenate(pairs, axis=-1)
    o_ref[0] = _rms(a, g_ref[...]).astype(BF16)


def _attention(q, k, v, kc, vc, gain, *, tq, tk, stable):
    b, _, s, _ = q.shape
    lc = kc.shape[2]
    per_batch = lambda i, j: (i, 0, 0, 0)
    return pl.pallas_call(
        functools.partial(_attn_kernel, tk=tk, stable=stable),
        grid=(b, s // tq),
        in_specs=[pl.BlockSpec((1, 4, tq, LANES), lambda i, j: (i, 0, j, 0)),
                  pl.BlockSpec((1, N_KV_HEADS, s, LANES), per_batch),
                  pl.BlockSpec((1, N_KV_HEADS, s, LANES), per_batch),
                  pl.BlockSpec((1, N_KV_HEADS, lc, LANES), per_batch),
                  pl.BlockSpec((1, N_KV_HEADS, lc, LANES), per_batch),
                  _resident((1, ATTN_DIM))],
        out_specs=pl.BlockSpec((1, tq, ATTN_DIM), lambda i, j: (i, j, 0)),
        out_shape=jax.ShapeDtypeStruct((b, s, ATTN_DIM), BF16),
        compiler_params=pltpu.CompilerParams(dimension_semantics=("arbitrary", "arbitrary"),
                                             vmem_limit_bytes=VMEM_LIMIT),
        name="attention_stable" if stable else "attention",
    )(q, k, v, kc, vc, gain)


def _mix_out_kernel(x_ref, an_ref, gb_ref, cu_ref, cup_ref, cun_ref, mod_ref, cw_ref, cg_ref,
                    wmix_ref, g2_ref, win_ref, wout_ref, fg_ref, o_ref, act_ref):
    j = pl.program_id(1)
    tm = x_ref.shape[1]
    x = x_ref[0]
    mod = mod_ref[0]
    cu = cu_ref[0]
    row = jax.lax.broadcasted_iota(jnp.int32, cu.shape, 0)
    halo = cup_ref.shape[1]
    prev_row = jnp.where(j == 0, 0.0, cup_ref[0, halo - 1:halo, :])
    next_row = jnp.where(j == pl.num_programs(1) - 1, 0.0, cun_ref[0, 0:1, :])
    cu_prev = jnp.where(row == 0, prev_row, pltpu.roll(cu, 1, 0))
    cu_next = jnp.where(row == tm - 1, next_row, pltpu.roll(cu, tm - 1, 0))
    cw = cw_ref[...]
    sc = gb_ref[0] * (cu_prev * cw[0:1, :] + cu * cw[1:2, :] + cu_next * cw[2:3, :])
    scn = _rms(sc, cg_ref[...]).astype(BF16)
    attn_dim = an_ref.shape[2]
    mix = _dot(an_ref[0], wmix_ref[:attn_dim, :]) + _dot(scn, wmix_ref[attn_dim:, :])
    x = x + mod[5:6, :] * mix
    h = _norm_mod(x, g2_ref[...], mod, 6).astype(BF16)
    y = _swiglu(h, win_ref, wout_ref, act_ref)
    x = x + 0.5 * mod[8:9, :] * y
    o_ref[0] = _rms(x, fg_ref[...])


def _mix_out(x, an, gb, cu, mods, conv_w, conv_gain, w_mix, gain2, w_in, w_out, final_gain, *, tm, halo=8):
    b, s, d = x.shape
    d_ff = w_out.shape[0]
    conv_dim = cu.shape[2]
    tok = lambda i, j: (i, j, 0)
    per = tm // halo
    n_halo = s // halo
    return pl.pallas_call(
        _mix_out_kernel,
        grid=(b, s // tm),
        in_specs=[pl.BlockSpec((1, tm, d), tok),
                  pl.BlockSpec((1, tm, an.shape[2]), tok),
                  pl.BlockSpec((1, tm, conv_dim), tok),
                  pl.BlockSpec((1, tm, conv_dim), tok),
                  pl.BlockSpec((1, halo, conv_dim), lambda i, j: (i, jnp.maximum(j * per - 1, 0), 0)),
                  pl.BlockSpec((1, halo, conv_dim), lambda i, j: (i, jnp.minimum((j + 1) * per, n_halo - 1), 0)),
                  pl.BlockSpec((1, N_MOD, d), lambda i, j: (i, 0, 0)),
                  _resident(conv_w.shape),
                  _resident((1, conv_dim)),
                  _resident(w_mix.shape),
                  _resident((1, d)),
                  _resident(w_in.shape),
                  _resident(w_out.shape),
                  _resident((1, d))],
        out_specs=pl.BlockSpec((1, tm, d), tok),
        out_shape=jax.ShapeDtypeStruct(x.shape, F32),
        scratch_shapes=[pltpu.VMEM((tm, d_ff), BF16)],
        compiler_params=pltpu.CompilerParams(dimension_semantics=("arbitrary", "arbitrary"),
                                             vmem_limit_bytes=VMEM_LIMIT),
        name="mix_out_ffn2",
    )(x, an, gb, cu, cu, cu, mods, conv_w, conv_gain, w_mix, gain2, w_in, w_out, final_gain)


def _rope_tables(n_tokens):
    rows = n_tokens // GRID_W
    row = jnp.repeat(jnp.arange(rows, dtype=F32), GRID_W)
    col = jnp.tile(jnp.arange(GRID_W, dtype=F32), rows)
    n_freq = HEAD_DIM // 4
    inv = ROPE_THETA ** (-jnp.arange(n_freq, dtype=F32) / n_freq)
    ang = jnp.concatenate([row[:, None] * inv, col[:, None] * inv], axis=-1)
    cos = jnp.repeat(jnp.cos(ang), 2, axis=-1)
    sign = jnp.tile(jnp.array([-1.0, 1.0], F32), HEAD_DIM // 2)
    sin = jnp.repeat(jnp.sin(ang), 2, axis=-1) * sign
    return jnp.tile(cos, (1, LANES // HEAD_DIM)), jnp.tile(sin, (1, LANES // HEAD_DIM))


def kernel(x, c, ctx, c_ctx, w_ada, b_ada, norm_ffn1, w_ffn1_in, w_ffn1_out, norm_mix, w_mix_in,
           q_norm, k_norm, conv_w, attn_out_norm, conv_out_norm, w_mix_out, norm_ffn2, w_ffn2_in,
           w_ffn2_out, final_norm):
    b, s, d = x.shape
    lc = ctx.shape[1]
    assert w_ada.shape[0] == 1, "single-layer problem"
    row = lambda v: v.reshape(1, -1)

    n_rows = 16
    cc = jnp.concatenate([c, c_ctx[None], jnp.zeros((n_rows - b - 1, d), F32)], axis=0)
    mall = _ada_mod(cc, w_ada[0], row(b_ada[0]))
    mods = mall[:b].reshape(b, N_MOD, d)
    mods_c = jnp.broadcast_to(mall[b].reshape(1, N_MOD, d), (b, N_MOD, d))

    w1_in, w1_out = w_ffn1_in[0].astype(BF16), w_ffn1_out[0].astype(BF16)
    w2_in, w2_out = w_ffn2_in[0].astype(BF16), w_ffn2_out[0].astype(BF16)
    w_in, w_mo = w_mix_in[0].astype(BF16), w_mix_out[0].astype(BF16)

    x1 = _ffn(x, mods, row(norm_ffn1[0]), w1_in, w1_out, base=0, tm=512)
    c1 = _ffn(ctx, mods_c, row(norm_ffn1[0]), w1_in, w1_out, base=0, tm=lc)

    cos, sin = _rope_tables(s)
    qg = jnp.tile(q_norm[0], N_Q_HEADS).reshape(1, ATTN_DIM) * (HEAD_DIM ** -0.5 * LOG2E)
    kg = jnp.tile(k_norm[0], N_KV_HEADS).reshape(1, KV_DIM)
    head_id = jnp.arange(2 * LANES) // HEAD_DIM
    ones = (head_id[:, None] == head_id[None, :]).astype(BF16)

    q, k, v, gb, cu = _mix_in(x1, mods, row(norm_mix[0]), w_in, qg, kg, ones, cos, sin,
                              base=3, tm=512, latent=True)
    kc, vc = _mix_in(c1, mods_c, row(norm_mix[0]), w_in, None, kg, ones, None, None,
                     base=3, tm=lc, latent=False)

    score_bound = (HEAD_DIM ** 0.5 * LOG2E) * jnp.max(jnp.abs(q_norm[0])) * jnp.max(jnp.abs(k_norm[0]))
    attend = lambda stable: functools.partial(_attention, tq=128, tk=512, stable=stable)
    an = jax.lax.cond(score_bound <= MAX_EXP2_ARG, attend(False), attend(True),
                      q, k, v, kc, vc, row(attn_out_norm[0]))

    return _mix_out(x1, an, gb, cu, mods, conv_w[0], row(conv_out_norm[0]), w_mo, row(norm_ffn2[0]),
                    w2_in, w2_out, row(final_norm), tm=512)
```

```python
import functools
import math

import jax
import jax.numpy as jnp
from jax.experimental import pallas as pl
from jax.experimental.pallas import tpu as pltpu

F32 = jnp.float32
BF16 = jnp.bfloat16

EPS = 1e-6
ROPE_THETA = 10000.0
GRID_W = 64
HEAD_DIM = 64
N_Q_HEADS = 8
N_KV_HEADS = 2
ATTN_DIM = N_Q_HEADS * HEAD_DIM
KV_DIM = N_KV_HEADS * HEAD_DIM
N_MOD = 9
LANES = 128
FF_CHUNK = 256
VMEM_LIMIT = 56 * 1024 * 1024
LOG2E = math.log2(math.e)
MAX_EXP2_ARG = 60.0


def _dot(a, b):
    return jnp.dot(a, b, preferred_element_type=F32)


def _dot_nt(a, b):
    return jax.lax.dot_general(a, b, (((1,), (1,)), ((), ())), preferred_element_type=F32)


def _rms(x, gain_row):
    ms = jnp.mean(x * x, axis=-1, keepdims=True)
    return x * jax.lax.rsqrt(ms + EPS) * gain_row


def _norm_mod(x, gain_row, mod, base):
    shift = mod[base:base + 1, :]
    scale = mod[base + 1:base + 2, :]
    return _rms(x, gain_row) * (1.0 + scale) + shift


def _swiglu(h, win_ref, wout_ref, act_ref, rows):
    d_ff = wout_ref.shape[0]
    for c in range(d_ff // FF_CHUNK):
        lo = c * FF_CHUNK
        g = _dot(h, win_ref[:, lo:lo + FF_CHUNK])
        u = _dot(h, win_ref[:, d_ff + lo:d_ff + lo + FF_CHUNK])
        act_ref[rows, lo:lo + FF_CHUNK] = (g * jax.nn.sigmoid(g) * u).astype(BF16)
    return _dot(act_ref[rows, :], wout_ref[...])


def _sub_tiles(n_rows, n_sub):
    sub = n_rows // n_sub
    return [slice(r * sub, (r + 1) * sub) for r in range(n_sub)]


def _ada_kernel(c_ref, w_ref, b_ref, o_ref):
    c = c_ref[...]
    h = (c * jax.nn.sigmoid(c)).astype(BF16)
    o_ref[...] = _dot(h, w_ref[...].astype(BF16)) + b_ref[...]


def _ada_mod(cc, w_ada, b_ada, tn=1024):
    rows, d = cc.shape
    n = w_ada.shape[1]
    return pl.pallas_call(
        _ada_kernel,
        grid=(n // tn,),
        in_specs=[pl.BlockSpec((rows, d), lambda j: (0, 0)),
                  pl.BlockSpec((d, tn), lambda j: (0, j)),
                  pl.BlockSpec((1, tn), lambda j: (0, j))],
        out_specs=pl.BlockSpec((rows, tn), lambda j: (0, j)),
        out_shape=jax.ShapeDtypeStruct((rows, n), F32),
        compiler_params=pltpu.CompilerParams(dimension_semantics=("arbitrary",),
                                             vmem_limit_bytes=VMEM_LIMIT),
        name="ada_mod",
    )(cc, w_ada, b_ada)


def _ffn_kernel(x_ref, mod_ref, g_ref, win_ref, wout_ref, o_ref, act_ref, *, base, n_sub):
    mod = mod_ref[0]
    for rows in _sub_tiles(x_ref.shape[1], n_sub):
        x = x_ref[0, rows, :]
        h = _norm_mod(x, g_ref[...], mod, base).astype(BF16)
        y = _swiglu(h, win_ref, wout_ref, act_ref, rows)
        o_ref[0, rows, :] = x + 0.5 * mod[base + 2:base + 3, :] * y


def _resident(shape):
    return pl.BlockSpec(shape, lambda *_: (0,) * len(shape), pipeline_mode=pl.Buffered(1))


def _ffn(x, mods, gain, w_in, w_out, *, base, tm, n_sub):
    b, s, d = x.shape
    d_ff = w_out.shape[0]
    return pl.pallas_call(
        functools.partial(_ffn_kernel, base=base, n_sub=n_sub),
        grid=(b, s // tm),
        in_specs=[pl.BlockSpec((1, tm, d), lambda i, j: (i, j, 0)),
                  pl.BlockSpec((1, N_MOD, d), lambda i, j: (i, 0, 0)),
                  _resident((1, d)),
                  _resident(w_in.shape),
                  _resident(w_out.shape)],
        out_specs=pl.BlockSpec((1, tm, d), lambda i, j: (i, j, 0)),
        out_shape=jax.ShapeDtypeStruct(x.shape, F32),
        scratch_shapes=[pltpu.VMEM((tm, d_ff), BF16)],
        compiler_params=pltpu.CompilerParams(dimension_semantics=("arbitrary", "arbitrary"),
                                             vmem_limit_bytes=VMEM_LIMIT),
        name="ffn",
    )(x, mods, gain, w_in, w_out)


def _pair_swap(x, even_lane):
    n = x.shape[-1]
    return jnp.where(even_lane, pltpu.roll(x, n - 1, 1), pltpu.roll(x, 1, 1))


def _head_sumsq(x, ones_ref):
    width = min(x.shape[-1], ones_ref.shape[0])
    ones = ones_ref[:width, :width]
    parts = []
    for j in range(x.shape[-1] // width):
        xj = x[:, j * width:(j + 1) * width]
        sq = xj * xj
        hi = sq.astype(BF16)
        lo = (sq - hi.astype(F32)).astype(BF16)
        parts.append(_dot(hi, ones) + _dot(lo, ones))
    return parts[0] if len(parts) == 1 else jnp.concatenate(parts, axis=-1)


def _qk_heads(p, gain_row, ones_ref):
    ss = _head_sumsq(p, ones_ref)
    return p * jax.lax.rsqrt(ss * (1.0 / HEAD_DIM) + EPS) * gain_row


def _mix_in_kernel(*refs, base, latent):
    if latent:
        (x_ref, mod_ref, g_ref, w_ref, qg_ref, kg_ref, ones_ref, cos_ref, sin_ref,
         q_ref, k_ref, v_ref, gb_ref, cu_ref) = refs
    else:
        x_ref, mod_ref, g_ref, w_ref, kg_ref, ones_ref, k_ref, v_ref = refs
    x = x_ref[0]
    h = _norm_mod(x, g_ref[...], mod_ref[0], base).astype(BF16)
    lane = jax.lax.broadcasted_iota(jnp.int32, (x.shape[0], LANES), 1)
    low_half = lane < HEAD_DIM

    off_k, off_v, off_b = ATTN_DIM, ATTN_DIM + KV_DIM, ATTN_DIM + 2 * KV_DIM
    conv_dim = (w_ref.shape[1] - off_b) // 3

    first = 0 if latent else off_k
    last = w_ref.shape[1] if latent else off_b
    proj = _dot(h, w_ref[:, first:last])
    seg = lambda lo, hi: proj[:, lo - first:hi - first]
    k = _qk_heads(seg(off_k, off_v), kg_ref[...], ones_ref)
    if latent:
        even = (lane & 1) == 0
        cos = cos_ref[...]
        sin = sin_ref[...]
        k = k * cos + _pair_swap(k, even) * sin
    kr = pltpu.roll(k, HEAD_DIM, 1)
    k_ref[0, 0] = jnp.where(low_half, k, kr).astype(BF16)
    k_ref[0, 1] = jnp.where(low_half, kr, k).astype(BF16)

    v = seg(off_v, off_b)
    vr = pltpu.roll(v, HEAD_DIM, 1)
    ones_col = jnp.where(lane == HEAD_DIM, 1.0, 0.0)
    v_ref[0, 0] = jnp.where(low_half, v, ones_col).astype(BF16)
    v_ref[0, 1] = jnp.where(low_half, vr, ones_col).astype(BF16)

    if latent:
        q = _qk_heads(seg(0, off_k), qg_ref[...], ones_ref)
        for j in range(ATTN_DIM // LANES):
            qj = q[:, j * LANES:(j + 1) * LANES]
            q_ref[0, j] = (qj * cos + _pair_swap(qj, even) * sin).astype(BF16)
        gb_ref[0] = seg(off_b, off_b + conv_dim)
        gc = seg(off_b + conv_dim, off_b + 2 * conv_dim)
        u = seg(off_b + 2 * conv_dim, off_b + 3 * conv_dim)
        cu_ref[0] = gc * u


def _mix_in(x, mods, gain, w, qg, kg, ones, cos, sin, *, base, tm, latent):
    b, s, d = x.shape
    conv_dim = (w.shape[1] - ATTN_DIM - 2 * KV_DIM) // 3
    tok = lambda i, j: (i, j, 0)
    head = lambda i, j: (i, 0, j, 0)
    in_specs = [pl.BlockSpec((1, tm, d), tok),
                pl.BlockSpec((1, N_MOD, d), lambda i, j: (i, 0, 0)),
                _resident((1, d)),
                _resident(w.shape)]
    args = [x, mods, gain, w]
    if latent:
        in_specs.append(_resident((1, ATTN_DIM)))
        args.append(qg)
    in_specs += [_resident((1, LANES)), _resident(ones.shape)]
    args += [kg, ones]
    out_specs = [pl.BlockSpec((1, N_KV_HEADS, tm, LANES), head)] * 2
    out_shape = [jax.ShapeDtypeStruct((b, N_KV_HEADS, s, LANES), BF16)] * 2
    if latent:
        in_specs += [pl.BlockSpec((tm, LANES), lambda i, j: (j, 0))] * 2
        args += [cos, sin]
        out_specs = [pl.BlockSpec((1, 4, tm, LANES), head)] + out_specs + [
            pl.BlockSpec((1, tm, conv_dim), tok)] * 2
        out_shape = [jax.ShapeDtypeStruct((b, 4, s, LANES), BF16)] + out_shape + [
            jax.ShapeDtypeStruct((b, s, conv_dim), F32)] * 2
    return pl.pallas_call(
        functools.partial(_mix_in_kernel, base=base, latent=latent),
        grid=(b, s // tm),
        in_specs=in_specs,
        out_specs=out_specs,
        out_shape=out_shape,
        compiler_params=pltpu.CompilerParams(dimension_semantics=("arbitrary", "arbitrary"),
                                             vmem_limit_bytes=VMEM_LIMIT),
        name="mix_in_latent" if latent else "mix_in_context",
    )(*args)


def _attn_kernel(q_ref, k_ref, v_ref, kc_ref, vc_ref, g_ref, o_ref, *, tk, stable):
    tq = q_ref.shape[2]
    n_chunks = k_ref.shape[2] // tk
    low2 = jax.lax.broadcasted_iota(jnp.int32, (2 * tq, LANES), 1) < HEAD_DIM
    low1 = jax.lax.broadcasted_iota(jnp.int32, (tq, LANES), 1) < HEAD_DIM

    pairs = []
    for kv in range(N_KV_HEADS):
        q2 = q_ref[0, 2 * kv:2 * kv + 2].reshape(2 * tq, LANES).astype(F32)
        q4 = jnp.concatenate([jnp.where(low2, q2, 0.0), jnp.where(low2, 0.0, q2)], axis=0).astype(BF16)
        chunks = [(k_ref[0, kv, c * tk:(c + 1) * tk, :], v_ref[0, kv, c * tk:(c + 1) * tk, :])
                  for c in range(n_chunks)]
        chunks.append((kc_ref[0, kv], vc_ref[0, kv]))
        m = acc = None
        for k_c, v_c in chunks:
            s = _dot_nt(q4, k_c)
            if stable:
                m_c = jnp.max(s, axis=-1, keepdims=True)
                m_new = m_c if m is None else jnp.maximum(m, m_c)
                pv = _dot(jnp.exp2(s - m_new).astype(BF16), v_c)
                acc = pv if acc is None else acc * jnp.exp2(m - m_new) + pv
                m = m_new
            else:
                pv = _dot(jnp.exp2(s).astype(BF16), v_c)
                acc = pv if acc is None else acc + pv
        o = acc / acc[:, HEAD_DIM:HEAD_DIM + 1]
        h0, h2, h1, h3 = (o[i * tq:(i + 1) * tq] for i in range(4))
        pairs.append(jnp.where(low1, h0, pltpu.roll(h1, HEAD_DIM, 1)))
        pairs.append(jnp.where(low1, h2, pltpu.roll(h3, HEAD_DIM, 1)))
    a = jnp.concatenate(pairs, axis=-1)
    o_ref[0] = _rms(a, g_ref[...]).astype(BF16)


def _attention(q, k, v, kc, vc, gain, *, tq, tk, stable):
    b, _, s, _ = q.shape
    lc = kc.shape[2]
    per_batch = lambda i, j: (i, 0, 0, 0)
    return pl.pallas_call(
        functools.partial(_attn_kernel, tk=tk, stable=stable),
        grid=(b, s // tq),
        in_specs=[pl.BlockSpec((1, 4, tq, LANES), lambda i, j: (i, 0, j, 0)),
                  pl.BlockSpec((1, N_KV_HEADS, s, LANES), per_batch),
                  pl.BlockSpec((1, N_KV_HEADS, s, LANES), per_batch),
                  pl.BlockSpec((1, N_KV_HEADS, lc, LANES), per_batch),
                  pl.BlockSpec((1, N_KV_HEADS, lc, LANES), per_batch),
                  _resident((1, ATTN_DIM))],
        out_specs=pl.BlockSpec((1, tq, ATTN_DIM), lambda i, j: (i, j, 0)),
        out_shape=jax.ShapeDtypeStruct((b, s, ATTN_DIM), BF16),
        compiler_params=pltpu.CompilerParams(dimension_semantics=("arbitrary", "arbitrary"),
                                             vmem_limit_bytes=VMEM_LIMIT),
        name="attention_stable" if stable else "attention",
    )(q, k, v, kc, vc, gain)


def _mix_out_kernel(x_ref, an_ref, gb_ref, cu_ref, cup_ref, cun_ref, mod_ref, cw_ref, cg_ref,
                    wmix_ref, g2_ref, win_ref, wout_ref, fg_ref, o_ref, act_ref, *, n_sub):
    j = pl.program_id(1)
    tm = x_ref.shape[1]
    mod = mod_ref[0]
    cw = cw_ref[...]
    halo = cup_ref.shape[1]
    attn_dim = an_ref.shape[2]
    for rows in _sub_tiles(tm, n_sub):
        lo, hi = rows.start, rows.stop
        n = hi - lo
        cu = cu_ref[0, rows, :]
        if lo == 0:
            prev_row = jnp.where(j == 0, 0.0, cup_ref[0, halo - 1:halo, :])
        else:
            prev_row = cu_ref[0, lo - 1:lo, :]
        if hi == tm:
            next_row = jnp.where(j == pl.num_programs(1) - 1, 0.0, cun_ref[0, 0:1, :])
        else:
            next_row = cu_ref[0, hi:hi + 1, :]
        row = jax.lax.broadcasted_iota(jnp.int32, cu.shape, 0)
        cu_prev = jnp.where(row == 0, prev_row, pltpu.roll(cu, 1, 0))
        cu_next = jnp.where(row == n - 1, next_row, pltpu.roll(cu, n - 1, 0))
        sc = gb_ref[0, rows, :] * (cu_prev * cw[0:1, :] + cu * cw[1:2, :] + cu_next * cw[2:3, :])
        scn = _rms(sc, cg_ref[...]).astype(BF16)
        mix = _dot(an_ref[0, rows, :], wmix_ref[:attn_dim, :]) + _dot(scn, wmix_ref[attn_dim:, :])
        x = x_ref[0, rows, :] + mod[5:6, :] * mix
        h = _norm_mod(x, g2_ref[...], mod, 6).astype(BF16)
        y = _swiglu(h, win_ref, wout_ref, act_ref, rows)
        x = x + 0.5 * mod[8:9, :] * y
        o_ref[0, rows, :] = _rms(x, fg_ref[...])


def _mix_out(x, an, gb, cu, mods, conv_w, conv_gain, w_mix, gain2, w_in, w_out, final_gain, *, tm, n_sub,
             halo=8):
    b, s, d = x.shape
    d_ff = w_out.shape[0]
    conv_dim = cu.shape[2]
    tok = lambda i, j: (i, j, 0)
    per = tm // halo
    n_halo = s // halo
    return pl.pallas_call(
        functools.partial(_mix_out_kernel, n_sub=n_sub),
        grid=(b, s // tm),
        in_specs=[pl.BlockSpec((1, tm, d), tok),
                  pl.BlockSpec((1, tm, an.shape[2]), tok),
                  pl.BlockSpec((1, tm, conv_dim), tok),
                  pl.BlockSpec((1, tm, conv_dim), tok),
                  pl.BlockSpec((1, halo, conv_dim), lambda i, j: (i, jnp.maximum(j * per - 1, 0), 0)),
                  pl.BlockSpec((1, halo, conv_dim), lambda i, j: (i, jnp.minimum((j + 1) * per, n_halo - 1), 0)),
                  pl.BlockSpec((1, N_MOD, d), lambda i, j: (i, 0, 0)),
                  _resident(conv_w.shape),
                  _resident((1, conv_dim)),
                  _resident(w_mix.shape),
                  _resident((1, d)),
                  _resident(w_in.shape),
                  _resident(w_out.shape),
                  _resident((1, d))],
        out_specs=pl.BlockSpec((1, tm, d), tok),
        out_shape=jax.ShapeDtypeStruct(x.shape, F32),
        scratch_shapes=[pltpu.VMEM((tm, d_ff), BF16)],
        compiler_params=pltpu.CompilerParams(dimension_semantics=("arbitrary", "arbitrary"),
                                             vmem_limit_bytes=VMEM_LIMIT),
        name="mix_out_ffn2",
    )(x, an, gb, cu, cu, cu, mods, conv_w, conv_gain, w_mix, gain2, w_in, w_out, final_gain)


def _rope_tables(n_tokens):
    rows = n_tokens // GRID_W
    row = jnp.repeat(jnp.arange(rows, dtype=F32), GRID_W)
    col = jnp.tile(jnp.arange(GRID_W, dtype=F32), rows)
    n_freq = HEAD_DIM // 4
    inv = ROPE_THETA ** (-jnp.arange(n_freq, dtype=F32) / n_freq)
    ang = jnp.concatenate([row[:, None] * inv, col[:, None] * inv], axis=-1)
    cos = jnp.repeat(jnp.cos(ang), 2, axis=-1)
    sign = jnp.tile(jnp.array([-1.0, 1.0], F32), HEAD_DIM // 2)
    sin = jnp.repeat(jnp.sin(ang), 2, axis=-1) * sign
    return jnp.tile(cos, (1, LANES // HEAD_DIM)), jnp.tile(sin, (1, LANES // HEAD_DIM))


def kernel(x, c, ctx, c_ctx, w_ada, b_ada, norm_ffn1, w_ffn1_in, w_ffn1_out, norm_mix, w_mix_in,
           q_norm, k_norm, conv_w, attn_out_norm, conv_out_norm, w_mix_out, norm_ffn2, w_ffn2_in,
           w_ffn2_out, final_norm):
    b, s, d = x.shape
    lc = ctx.shape[1]
    assert w_ada.shape[0] == 1, "single-layer problem"
    row = lambda v: v.reshape(1, -1)

    n_rows = 16
    cc = jnp.concatenate([c, c_ctx[None], jnp.zeros((n_rows - b - 1, d), F32)], axis=0)
    mall = _ada_mod(cc, w_ada[0], row(b_ada[0]))
    mods = mall[:b].reshape(b, N_MOD, d)
    mods_c = jnp.broadcast_to(mall[b].reshape(1, N_MOD, d), (b, N_MOD, d))

    w1_in, w1_out = w_ffn1_in[0].astype(BF16), w_ffn1_out[0].astype(BF16)
    w2_in, w2_out = w_ffn2_in[0].astype(BF16), w_ffn2_out[0].astype(BF16)
    w_in, w_mo = w_mix_in[0].astype(BF16), w_mix_out[0].astype(BF16)

    x1 = _ffn(x, mods, row(norm_ffn1[0]), w1_in, w1_out, base=0, tm=1024, n_sub=2)
    c1 = _ffn(ctx, mods_c, row(norm_ffn1[0]), w1_in, w1_out, base=0, tm=lc, n_sub=1)

    cos, sin = _rope_tables(s)
    qg = jnp.tile(q_norm[0], N_Q_HEADS).reshape(1, ATTN_DIM) * (HEAD_DIM ** -0.5 * LOG2E)
    kg = jnp.tile(k_norm[0], N_KV_HEADS).reshape(1, KV_DIM)
    head_id = jnp.arange(2 * LANES) // HEAD_DIM
    ones = (head_id[:, None] == head_id[None, :]).astype(BF16)

    q, k, v, gb, cu = _mix_in(x1, mods, row(norm_mix[0]), w_in, qg, kg, ones, cos, sin,
                              base=3, tm=512, latent=True)
    kc, vc = _mix_in(c1, mods_c, row(norm_mix[0]), w_in, None, kg, ones, None, None,
                     base=3, tm=lc, latent=False)

    score_bound = (HEAD_DIM ** 0.5 * LOG2E) * jnp.max(jnp.abs(q_norm[0])) * jnp.max(jnp.abs(k_norm[0]))
    attend = lambda stable: functools.partial(_attention, tq=256, tk=512, stable=stable)
    an = jax.lax.cond(score_bound <= MAX_EXP2_ARG, attend(False), attend(True),
                      q, k, v, kc, vc, row(attn_out_norm[0]))

    return _mix_out(x1, an, gb, cu, mods, conv_w[0], row(conv_out_norm[0]), w_mo, row(norm_ffn2[0]),
                    w2_in, w2_out, row(final_norm), tm=1024, n_sub=2)
```

```python
import functools
import math

import jax
import jax.numpy as jnp
from jax.experimental import pallas as pl
from jax.experimental.pallas import tpu as pltpu

F32 = jnp.float32
BF16 = jnp.bfloat16

EPS = 1e-6
ROPE_THETA = 10000.0
GRID_W = 64
HEAD_DIM = 64
N_Q_HEADS = 8
N_KV_HEADS = 2
ATTN_DIM = N_Q_HEADS * HEAD_DIM
KV_DIM = N_KV_HEADS * HEAD_DIM
N_MOD = 9
LANES = 128
FF_CHUNK = 256
VMEM_LIMIT = 56 * 1024 * 1024
LOG2E = math.log2(math.e)
MAX_EXP2_ARG = 60.0


def _dot(a, b):
    return jnp.dot(a, b, preferred_element_type=F32)


def _dot_nt(a, b):
    return jax.lax.dot_general(a, b, (((1,), (1,)), ((), ())), preferred_element_type=F32)


def _rms(x, gain_row):
    ms = jnp.mean(x * x, axis=-1, keepdims=True)
    return x * jax.lax.rsqrt(ms + EPS) * gain_row


def _norm_mod(x, gain_row, mod, base):
    shift = mod[base:base + 1, :]
    scale = mod[base + 1:base + 2, :]
    return _rms(x, gain_row) * (1.0 + scale) + shift


def _swiglu(h, win_ref, wout_ref, act_ref, rows):
    d_ff = wout_ref.shape[0]
    for c in range(d_ff // FF_CHUNK):
        lo = c * FF_CHUNK
        g = _dot(h, win_ref[:, lo:lo + FF_CHUNK])
        u = _dot(h, win_ref[:, d_ff + lo:d_ff + lo + FF_CHUNK])
        act_ref[rows, lo:lo + FF_CHUNK] = (g * jax.nn.sigmoid(g) * u).astype(BF16)
    return _dot(act_ref[rows, :], wout_ref[...])


def _sub_tiles(n_rows, n_sub):
    sub = n_rows // n_sub
    return [slice(r * sub, (r + 1) * sub) for r in range(n_sub)]


def _ada_kernel(c_ref, w_ref, b_ref, o_ref):
    c = c_ref[...]
    h = (c * jax.nn.sigmoid(c)).astype(BF16)
    o_ref[...] = _dot(h, w_ref[...].astype(BF16)) + b_ref[...]


def _ada_mod(cc, w_ada, b_ada, tn=1024):
    rows, d = cc.shape
    n = w_ada.shape[1]
    return pl.pallas_call(
        _ada_kernel,
        grid=(n // tn,),
        in_specs=[pl.BlockSpec((rows, d), lambda j: (0, 0)),
                  pl.BlockSpec((d, tn), lambda j: (0, j)),
                  pl.BlockSpec((1, tn), lambda j: (0, j))],
        out_specs=pl.BlockSpec((rows, tn), lambda j: (0, j)),
        out_shape=jax.ShapeDtypeStruct((rows, n), F32),
        compiler_params=pltpu.CompilerParams(dimension_semantics=("arbitrary",),
                                             vmem_limit_bytes=VMEM_LIMIT),
        name="ada_mod",
    )(cc, w_ada, b_ada)


def _ffn_kernel(x_ref, mod_ref, g_ref, win_ref, wout_ref, o_ref, act_ref, *, base, n_sub):
    mod = mod_ref[0]
    for rows in _sub_tiles(x_ref.shape[1], n_sub):
        x = x_ref[0, rows, :]
        h = _norm_mod(x, g_ref[...], mod, base).astype(BF16)
        y = _swiglu(h, win_ref, wout_ref, act_ref, rows)
        o_ref[0, rows, :] = x + 0.5 * mod[base + 2:base + 3, :] * y


def _resident(shape):
    return pl.BlockSpec(shape, lambda *_: (0,) * len(shape), pipeline_mode=pl.Buffered(1))


def _ffn(x, mods, gain, w_in, w_out, *, base, tm, n_sub):
    b, s, d = x.shape
    d_ff = w_out.shape[0]
    return pl.pallas_call(
        functools.partial(_ffn_kernel, base=base, n_sub=n_sub),
        grid=(b, s // tm),
        in_specs=[pl.BlockSpec((1, tm, d), lambda i, j: (i, j, 0)),
                  pl.BlockSpec((1, N_MOD, d), lambda i, j: (i, 0, 0)),
                  _resident((1, d)),
                  _resident(w_in.shape),
                  _resident(w_out.shape)],
        out_specs=pl.BlockSpec((1, tm, d), lambda i, j: (i, j, 0)),
        out_shape=jax.ShapeDtypeStruct(x.shape, F32),
        scratch_shapes=[pltpu.VMEM((tm, d_ff), BF16)],
        compiler_params=pltpu.CompilerParams(dimension_semantics=("arbitrary", "arbitrary"),
                                             vmem_limit_bytes=VMEM_LIMIT),
        name="ffn",
    )(x, mods, gain, w_in, w_out)


def _pair_swap(x, even_lane):
    n = x.shape[-1]
    return jnp.where(even_lane, pltpu.roll(x, n - 1, 1), pltpu.roll(x, 1, 1))


def _head_sumsq(x, ones_ref):
    width = ones_ref.shape[0]
    ones = ones_ref[...]
    parts = []
    for j in range(x.shape[-1] // width):
        xj = x[:, j * width:(j + 1) * width]
        sq = xj * xj
        hi = sq.astype(BF16)
        lo = (sq - hi.astype(F32)).astype(BF16)
        parts.append(_dot(hi, ones) + _dot(lo, ones))
    return parts[0] if len(parts) == 1 else jnp.concatenate(parts, axis=-1)


def _qk_heads(p, ss, gain_row):
    return p * jax.lax.rsqrt(ss * (1.0 / HEAD_DIM) + EPS) * gain_row


def _mix_in_kernel(*refs, base, latent):
    if latent:
        (x_ref, mod_ref, g_ref, w_ref, qg_ref, kg_ref, ones_ref, cos_ref, sin_ref,
         q_ref, k_ref, v_ref, gb_ref, cu_ref) = refs
    else:
        x_ref, mod_ref, g_ref, w_ref, kg_ref, ones_ref, k_ref, v_ref = refs
    x = x_ref[0]
    h = _norm_mod(x, g_ref[...], mod_ref[0], base).astype(BF16)
    lane = jax.lax.broadcasted_iota(jnp.int32, (x.shape[0], LANES), 1)
    low_half = lane < HEAD_DIM

    off_k, off_v, off_b = ATTN_DIM, ATTN_DIM + KV_DIM, ATTN_DIM + 2 * KV_DIM
    conv_dim = (w_ref.shape[1] - off_b) // 3

    first = 0 if latent else off_k
    proj = _dot(h, w_ref[:, first:off_b])
    seg = lambda lo, hi: proj[:, lo - first:hi - first]
    sumsq = _head_sumsq(seg(first, off_b), ones_ref)
    seg_ss = lambda lo, hi: sumsq[:, lo - first:hi - first]
    k = _qk_heads(seg(off_k, off_v), seg_ss(off_k, off_v), kg_ref[...])
    if latent:
        even = (lane & 1) == 0
        cos = cos_ref[...]
        sin = sin_ref[...]
        k = k * cos + _pair_swap(k, even) * sin
    kr = pltpu.roll(k, HEAD_DIM, 1)
    k_ref[0, 0] = jnp.where(low_half, k, kr).astype(BF16)
    k_ref[0, 1] = jnp.where(low_half, kr, k).astype(BF16)

    v = seg(off_v, off_b)
    vr = pltpu.roll(v, HEAD_DIM, 1)
    ones_col = jnp.where(lane == HEAD_DIM, 1.0, 0.0)
    v_ref[0, 0] = jnp.where(low_half, v, ones_col).astype(BF16)
    v_ref[0, 1] = jnp.where(low_half, vr, ones_col).astype(BF16)

    if latent:
        q = _qk_heads(seg(0, off_k), seg_ss(0, off_k), qg_ref[...])
        for j in range(ATTN_DIM // LANES):
            qj = q[:, j * LANES:(j + 1) * LANES]
            q_ref[0, j] = (qj * cos + _pair_swap(qj, even) * sin).astype(BF16)
        conv = _dot(h, w_ref[:, off_b:])
        gb_ref[0] = conv[:, :conv_dim]
        gc = conv[:, conv_dim:2 * conv_dim]
        u = conv[:, 2 * conv_dim:]
        cu_ref[0] = gc * u


def _mix_in(x, mods, gain, w, qg, kg, ones, cos, sin, *, base, tm, latent):
    b, s, d = x.shape
    conv_dim = (w.shape[1] - ATTN_DIM - 2 * KV_DIM) // 3
    tok = lambda i, j: (i, j, 0)
    head = lambda i, j: (i, 0, j, 0)
    in_specs = [pl.BlockSpec((1, tm, d), tok),
                pl.BlockSpec((1, N_MOD, d), lambda i, j: (i, 0, 0)),
                _resident((1, d)),
                _resident(w.shape)]
    args = [x, mods, gain, w]
    if latent:
        in_specs.append(_resident((1, ATTN_DIM)))
        args.append(qg)
    in_specs += [_resident((1, LANES)), _resident(ones.shape)]
    args += [kg, ones]
    out_specs = [pl.BlockSpec((1, N_KV_HEADS, tm, LANES), head)] * 2
    out_shape = [jax.ShapeDtypeStruct((b, N_KV_HEADS, s, LANES), BF16)] * 2
    if latent:
        in_specs += [pl.BlockSpec((tm, LANES), lambda i, j: (j, 0))] * 2
        args += [cos, sin]
        out_specs = [pl.BlockSpec((1, 4, tm, LANES), head)] + out_specs + [
            pl.BlockSpec((1, tm, conv_dim), tok)] * 2
        out_shape = [jax.ShapeDtypeStruct((b, 4, s, LANES), BF16)] + out_shape + [
            jax.ShapeDtypeStruct((b, s, conv_dim), F32)] * 2
    return pl.pallas_call(
        functools.partial(_mix_in_kernel, base=base, latent=latent),
        grid=(b, s // tm),
        in_specs=in_specs,
        out_specs=out_specs,
        out_shape=out_shape,
        compiler_params=pltpu.CompilerParams(dimension_semantics=("arbitrary", "arbitrary"),
                                             vmem_limit_bytes=VMEM_LIMIT),
        name="mix_in_latent" if latent else "mix_in_context",
    )(*args)


def _attn_kernel(q_ref, k_ref, v_ref, kc_ref, vc_ref, g_ref, o_ref, *, tk, stable):
    tq = q_ref.shape[2]
    n_chunks = k_ref.shape[2] // tk
    low2 = jax.lax.broadcasted_iota(jnp.int32, (2 * tq, LANES), 1) < HEAD_DIM
    low1 = jax.lax.broadcasted_iota(jnp.int32, (tq, LANES), 1) < HEAD_DIM

    pairs = []
    for kv in range(N_KV_HEADS):
        q2 = q_ref[0, 2 * kv:2 * kv + 2].reshape(2 * tq, LANES).astype(F32)
        q4 = jnp.concatenate([jnp.where(low2, q2, 0.0), jnp.where(low2, 0.0, q2)], axis=0).astype(BF16)
        both = lambda ref, rows: jnp.concatenate([ref[0, g, rows, :] for g in range(N_KV_HEADS)], axis=-1)
        chunks = [(k_ref[0, kv, c * tk:(c + 1) * tk, :], both(v_ref, slice(c * tk, (c + 1) * tk)))
                  for c in range(n_chunks)]
        chunks.append((kc_ref[0, kv], both(vc_ref, slice(None))))
        m = acc = None
        for k_c, v_c in chunks:
            s = _dot_nt(q4, k_c)
            if stable:
                m_c = jnp.max(s, axis=-1, keepdims=True)
                m_new = m_c if m is None else jnp.maximum(m, m_c)
                pv = _dot(jnp.exp2(s - m_new).astype(BF16), v_c)[:, kv * LANES:(kv + 1) * LANES]
                acc = pv if acc is None else acc * jnp.exp2(m - m_new) + pv
                m = m_new
            else:
                pv = _dot(jnp.exp2(s).astype(BF16), v_c)[:, kv * LANES:(kv + 1) * LANES]
                acc = pv if acc is None else acc + pv
        o = acc / acc[:, HEAD_DIM:HEAD_DIM + 1]
        h0, h2, h1, h3 = (o[i * tq:(i + 1) * tq] for i in range(4))
        pairs.append(jnp.where(low1, h0, pltpu.roll(h1, HEAD_DIM, 1)))
        pairs.append(jnp.where(low1, h2, pltpu.roll(h3, HEAD_DIM, 1)))
    a = jnp.concatenate(pairs, axis=-1)
    o_ref[0] = _rms(a, g_ref[...]).astype(BF16)


def _attention(q, k, v, kc, vc, gain, *, tq, tk, stable):
    b, _, s, _ = q.shape
    lc = kc.shape[2]
    per_batch = lambda i, j: (i, 0, 0, 0)
    return pl.pallas_call(
        functools.partial(_attn_kernel, tk=tk, stable=stable),
        grid=(b, s // tq),
        in_specs=[pl.BlockSpec((1, 4, tq, LANES), lambda i, j: (i, 0, j, 0)),
                  pl.BlockSpec((1, N_KV_HEADS, s, LANES), per_batch),
                  pl.BlockSpec((1, N_KV_HEADS, s, LANES), per_batch),
                  pl.BlockSpec((1, N_KV_HEADS, lc, LANES), per_batch),
                  pl.BlockSpec((1, N_KV_HEADS, lc, LANES), per_batch),
                  _resident((1, ATTN_DIM))],
        out_specs=pl.BlockSpec((1, tq, ATTN_DIM), lambda i, j: (i, j, 0)),
        out_shape=jax.ShapeDtypeStruct((b, s, ATTN_DIM), BF16),
        compiler_params=pltpu.CompilerParams(dimension_semantics=("arbitrary", "arbitrary"),
                                             vmem_limit_bytes=VMEM_LIMIT),
        name="attention_stable" if stable else "attention",
    )(q, k, v, kc, vc, gain)


def _mix_out_kernel(x_ref, an_ref, gb_ref, cu_ref, cup_ref, cun_ref, mod_ref, cw_ref, cg_ref,
                    wmix_ref, g2_ref, win_ref, wout_ref, fg_ref, o_ref, act_ref, *, n_sub):
    j = pl.program_id(1)
    tm = x_ref.shape[1]
    mod = mod_ref[0]
    cw = cw_ref[...]
    halo = cup_ref.shape[1]
    attn_dim = an_ref.shape[2]
    for rows in _sub_tiles(tm, n_sub):
        lo, hi = rows.start, rows.stop
        n = hi - lo
        cu = cu_ref[0, rows, :]
        if lo == 0:
            prev_row = jnp.where(j == 0, 0.0, cup_ref[0, halo - 1:halo, :])
        else:
            prev_row = cu_ref[0, lo - 1:lo, :]
        if hi == tm:
            next_row = jnp.where(j == pl.num_programs(1) - 1, 0.0, cun_ref[0, 0:1, :])
        else:
            next_row = cu_ref[0, hi:hi + 1, :]
        row = jax.lax.broadcasted_iota(jnp.int32, cu.shape, 0)
        cu_prev = jnp.where(row == 0, prev_row, pltpu.roll(cu, 1, 0))
        cu_next = jnp.where(row == n - 1, next_row, pltpu.roll(cu, n - 1, 0))
        sc = gb_ref[0, rows, :] * (cu_prev * cw[0:1, :] + cu * cw[1:2, :] + cu_next * cw[2:3, :])
        scn = _rms(sc, cg_ref[...]).astype(BF16)
        mix = _dot(an_ref[0, rows, :], wmix_ref[:attn_dim, :]) + _dot(scn, wmix_ref[attn_dim:, :])
        x = x_ref[0, rows, :] + mod[5:6, :] * mix
        h = _norm_mod(x, g2_ref[...], mod, 6).astype(BF16)
        y = _swiglu(h, win_ref, wout_ref, act_ref, rows)
        x = x + 0.5 * mod[8:9, :] * y
        o_ref[0, rows, :] = _rms(x, fg_ref[...])


def _mix_out(x, an, gb, cu, mods, conv_w, conv_gain, w_mix, gain2, w_in, w_out, final_gain, *, tm, n_sub,
             halo=8):
    b, s, d = x.shape
    d_ff = w_out.shape[0]
    conv_dim = cu.shape[2]
    tok = lambda i, j: (i, j, 0)
    per = tm // halo
    n_halo = s // halo
    return pl.pallas_call(
        functools.partial(_mix_out_kernel, n_sub=n_sub),
        grid=(b, s // tm),
        in_specs=[pl.BlockSpec((1, tm, d), tok),
                  pl.BlockSpec((1, tm, an.shape[2]), tok),
                  pl.BlockSpec((1, tm, conv_dim), tok),
                  pl.BlockSpec((1, tm, conv_dim), tok),
                  pl.BlockSpec((1, halo, conv_dim), lambda i, j: (i, jnp.maximum(j * per - 1, 0), 0)),
                  pl.BlockSpec((1, halo, conv_dim), lambda i, j: (i, jnp.minimum((j + 1) * per, n_halo - 1), 0)),
                  pl.BlockSpec((1, N_MOD, d), lambda i, j: (i, 0, 0)),
                  _resident(conv_w.shape),
                  _resident((1, conv_dim)),
                  _resident(w_mix.shape),
                  _resident((1, d)),
                  _resident(w_in.shape),
                  _resident(w_out.shape),
                  _resident((1, d))],
        out_specs=pl.BlockSpec((1, tm, d), tok),
        out_shape=jax.ShapeDtypeStruct(x.shape, F32),
        scratch_shapes=[pltpu.VMEM((tm, d_ff), BF16)],
        compiler_params=pltpu.CompilerParams(dimension_semantics=("arbitrary", "arbitrary"),
                                             vmem_limit_bytes=VMEM_LIMIT),
        name="mix_out_ffn2",
    )(x, an, gb, cu, cu, cu, mods, conv_w, conv_gain, w_mix, gain2, w_in, w_out, final_gain)


def _rope_tables(n_tokens):
    rows = n_tokens // GRID_W
    row = jnp.repeat(jnp.arange(rows, dtype=F32), GRID_W)
    col = jnp.tile(jnp.arange(GRID_W, dtype=F32), rows)
    n_freq = HEAD_DIM // 4
    inv = ROPE_THETA ** (-jnp.arange(n_freq, dtype=F32) / n_freq)
    ang = jnp.concatenate([row[:, None] * inv, col[:, None] * inv], axis=-1)
    cos = jnp.repeat(jnp.cos(ang), 2, axis=-1)
    sign = jnp.tile(jnp.array([-1.0, 1.0], F32), HEAD_DIM // 2)
    sin = jnp.repeat(jnp.sin(ang), 2, axis=-1) * sign
    return jnp.tile(cos, (1, LANES // HEAD_DIM)), jnp.tile(sin, (1, LANES // HEAD_DIM))


def kernel(x, c, ctx, c_ctx, w_ada, b_ada, norm_ffn1, w_ffn1_in, w_ffn1_out, norm_mix, w_mix_in,
           q_norm, k_norm, conv_w, attn_out_norm, conv_out_norm, w_mix_out, norm_ffn2, w_ffn2_in,
           w_ffn2_out, final_norm):
    b, s, d = x.shape
    lc = ctx.shape[1]
    assert w_ada.shape[0] == 1, "single-layer problem"
    row = lambda v: v.reshape(1, -1)

    n_rows = 16
    cc = jnp.concatenate([c, c_ctx[None], jnp.zeros((n_rows - b - 1, d), F32)], axis=0)
    mall = _ada_mod(cc, w_ada[0], row(b_ada[0]))
    mods = mall[:b].reshape(b, N_MOD, d)
    mods_c = jnp.broadcast_to(mall[b].reshape(1, N_MOD, d), (b, N_MOD, d))

    w1_in, w1_out = w_ffn1_in[0].astype(BF16), w_ffn1_out[0].astype(BF16)
    w2_in, w2_out = w_ffn2_in[0].astype(BF16), w_ffn2_out[0].astype(BF16)
    w_in, w_mo = w_mix_in[0].astype(BF16), w_mix_out[0].astype(BF16)

    x1 = _ffn(x, mods, row(norm_ffn1[0]), w1_in, w1_out, base=0, tm=1024, n_sub=2)
    c1 = _ffn(ctx, mods_c, row(norm_ffn1[0]), w1_in, w1_out, base=0, tm=lc, n_sub=1)

    cos, sin = _rope_tables(s)
    qg = jnp.tile(q_norm[0], N_Q_HEADS).reshape(1, ATTN_DIM) * (HEAD_DIM ** -0.5 * LOG2E)
    kg = jnp.tile(k_norm[0], N_KV_HEADS).reshape(1, KV_DIM)
    head_id = jnp.arange(2 * LANES) // HEAD_DIM
    ones = (head_id[:, None] == head_id[None, :]).astype(BF16)

    q, k, v, gb, cu = _mix_in(x1, mods, row(norm_mix[0]), w_in, qg, kg, ones, cos, sin,
                              base=3, tm=512, latent=True)
    kc, vc = _mix_in(c1, mods_c, row(norm_mix[0]), w_in, None, kg, ones, None, None,
                     base=3, tm=lc, latent=False)

    score_bound = (HEAD_DIM ** 0.5 * LOG2E) * jnp.max(jnp.abs(q_norm[0])) * jnp.max(jnp.abs(k_norm[0]))
    attend = lambda stable: functools.partial(_attention, tq=256, tk=512, stable=stable)
    an = jax.lax.cond(score_bound <= MAX_EXP2_ARG, attend(False), attend(True),
                      q, k, v, kc, vc, row(attn_out_norm[0]))

    return _mix_out(x1, an, gb, cu, mods, conv_w[0], row(conv_out_norm[0]), w_mo, row(norm_ffn2[0]),
                    w2_in, w2_out, row(final_norm), tm=1024, n_sub=2)
```

```python
import functools
import math

import jax
import jax.numpy as jnp
from jax.experimental import pallas as pl
from jax.experimental.pallas import tpu as pltpu

F32 = jnp.float32
BF16 = jnp.bfloat16

EPS = 1e-6
ROPE_THETA = 10000.0
GRID_W = 64
HEAD_DIM = 64
N_Q_HEADS = 8
N_KV_HEADS = 2
ATTN_DIM = N_Q_HEADS * HEAD_DIM
KV_DIM = N_KV_HEADS * HEAD_DIM
N_MOD = 9
LANES = 128
FF_CHUNK = 256
VMEM_LIMIT = 56 * 1024 * 1024
LOG2E = math.log2(math.e)
MAX_EXP2_ARG = 60.0
V_ROWS = 80


def _dot(a, b):
    return jnp.dot(a, b, preferred_element_type=F32)


def _dot_nt(a, b):
    return jax.lax.dot_general(a, b, (((1,), (1,)), ((), ())), preferred_element_type=F32)


def _rms(x, gain_row):
    ms = jnp.mean(x * x, axis=-1, keepdims=True)
    return x * jax.lax.rsqrt(ms + EPS) * gain_row


def _norm_mod(x, gain_row, mod, base):
    shift = mod[base:base + 1, :]
    scale = mod[base + 1:base + 2, :]
    return _rms(x, gain_row) * (1.0 + scale) + shift


def _swiglu(h, win_ref, wout_ref, act_ref, rows):
    d_ff = wout_ref.shape[0]
    for c in range(d_ff // FF_CHUNK):
        lo = c * FF_CHUNK
        g = _dot(h, win_ref[:, lo:lo + FF_CHUNK])
        u = _dot(h, win_ref[:, d_ff + lo:d_ff + lo + FF_CHUNK])
        act_ref[rows, lo:lo + FF_CHUNK] = (g * jax.nn.sigmoid(g) * u).astype(BF16)
    return _dot(act_ref[rows, :], wout_ref[...])


def _sub_tiles(n_rows, n_sub):
    sub = n_rows // n_sub
    return [slice(r * sub, (r + 1) * sub) for r in range(n_sub)]


def _ada_kernel(c_ref, w_ref, b_ref, o_ref):
    c = c_ref[...]
    h = (c * jax.nn.sigmoid(c)).astype(BF16)
    o_ref[...] = _dot(h, w_ref[...].astype(BF16)) + b_ref[...]


def _ada_mod(cc, w_ada, b_ada, tn=1024):
    rows, d = cc.shape
    n = w_ada.shape[1]
    return pl.pallas_call(
        _ada_kernel,
        grid=(n // tn,),
        in_specs=[pl.BlockSpec((rows, d), lambda j: (0, 0)),
                  pl.BlockSpec((d, tn), lambda j: (0, j)),
                  pl.BlockSpec((1, tn), lambda j: (0, j))],
        out_specs=pl.BlockSpec((rows, tn), lambda j: (0, j)),
        out_shape=jax.ShapeDtypeStruct((rows, n), F32),
        compiler_params=pltpu.CompilerParams(dimension_semantics=("arbitrary",),
                                             vmem_limit_bytes=VMEM_LIMIT),
        name="ada_mod",
    )(cc, w_ada, b_ada)


def _ffn_kernel(x_ref, mod_ref, g_ref, win_ref, wout_ref, o_ref, act_ref, *, base, n_sub):
    mod = mod_ref[0]
    for rows in _sub_tiles(x_ref.shape[1], n_sub):
        x = x_ref[0, rows, :]
        h = _norm_mod(x, g_ref[...], mod, base).astype(BF16)
        y = _swiglu(h, win_ref, wout_ref, act_ref, rows)
        o_ref[0, rows, :] = x + 0.5 * mod[base + 2:base + 3, :] * y


def _resident(shape):
    return pl.BlockSpec(shape, lambda *_: (0,) * len(shape), pipeline_mode=pl.Buffered(1))


def _ffn(x, mods, gain, w_in, w_out, *, base, tm, n_sub):
    b, s, d = x.shape
    d_ff = w_out.shape[0]
    return pl.pallas_call(
        functools.partial(_ffn_kernel, base=base, n_sub=n_sub),
        grid=(b, s // tm),
        in_specs=[pl.BlockSpec((1, tm, d), lambda i, j: (i, j, 0)),
                  pl.BlockSpec((1, N_MOD, d), lambda i, j: (i, 0, 0)),
                  _resident((1, d)),
                  _resident(w_in.shape),
                  _resident(w_out.shape)],
        out_specs=pl.BlockSpec((1, tm, d), lambda i, j: (i, j, 0)),
        out_shape=jax.ShapeDtypeStruct(x.shape, F32),
        scratch_shapes=[pltpu.VMEM((tm, d_ff), BF16)],
        compiler_params=pltpu.CompilerParams(dimension_semantics=("arbitrary", "arbitrary"),
                                             vmem_limit_bytes=VMEM_LIMIT),
        name="ffn",
    )(x, mods, gain, w_in, w_out)


def _pair_swap(x, even_lane):
    n = x.shape[-1]
    return jnp.where(even_lane, pltpu.roll(x, n - 1, 1), pltpu.roll(x, 1, 1))


def _head_sumsq(x, ones_ref):
    width = ones_ref.shape[0]
    ones = ones_ref[...]
    parts = []
    for j in range(x.shape[-1] // width):
        xj = x[:, j * width:(j + 1) * width]
        sq = xj * xj
        hi = sq.astype(BF16)
        lo = (sq - hi.astype(F32)).astype(BF16)
        parts.append(_dot(hi, ones) + _dot(lo, ones))
    return parts[0] if len(parts) == 1 else jnp.concatenate(parts, axis=-1)


def _qk_heads(p, ss, gain_row):
    return p * jax.lax.rsqrt(ss * (1.0 / HEAD_DIM) + EPS) * gain_row


def _mix_in_kernel(*refs, base, latent):
    if latent:
        (x_ref, mod_ref, g_ref, w_ref, qg_ref, kg_ref, ones_ref, cos_ref, sin_ref,
         q_ref, k_ref, v_ref, gb_ref, cu_ref) = refs
    else:
        x_ref, mod_ref, g_ref, w_ref, kg_ref, ones_ref, k_ref, v_ref = refs
    x = x_ref[0]
    h = _norm_mod(x, g_ref[...], mod_ref[0], base).astype(BF16)
    lane = jax.lax.broadcasted_iota(jnp.int32, (x.shape[0], LANES), 1)
    low_half = lane < HEAD_DIM

    off_k, off_v, off_b = ATTN_DIM, ATTN_DIM + KV_DIM, ATTN_DIM + 2 * KV_DIM
    conv_dim = (w_ref.shape[1] - off_b) // 3

    first = 0 if latent else off_k
    proj = _dot(h, w_ref[:, first:off_b])
    seg = lambda lo, hi: proj[:, lo - first:hi - first]
    sumsq = _head_sumsq(seg(first, off_b), ones_ref)
    seg_ss = lambda lo, hi: sumsq[:, lo - first:hi - first]
    k = _qk_heads(seg(off_k, off_v), seg_ss(off_k, off_v), kg_ref[...])
    if latent:
        even = (lane & 1) == 0
        cos = cos_ref[...]
        sin = sin_ref[...]
        k = k * cos + _pair_swap(k, even) * sin
    kr = pltpu.roll(k, HEAD_DIM, 1)
    k_ref[0, 0] = jnp.where(low_half, k, kr).astype(BF16)
    k_ref[0, 1] = jnp.where(low_half, kr, k).astype(BF16)

    v = seg(off_v, off_b)
    vr = pltpu.roll(v, HEAD_DIM, 1)
    ones_col = jnp.where(lane == HEAD_DIM, 1.0, 0.0)
    v_ref[0, 0] = jnp.where(low_half, v, ones_col).T[:V_ROWS].astype(BF16)
    v_ref[0, 1] = jnp.where(low_half, vr, ones_col).T[:V_ROWS].astype(BF16)

    if latent:
        q = _qk_heads(seg(0, off_k), seg_ss(0, off_k), qg_ref[...])
        for j in range(ATTN_DIM // LANES):
            qj = q[:, j * LANES:(j + 1) * LANES]
            q_ref[0, j] = (qj * cos + _pair_swap(qj, even) * sin).astype(BF16)
        conv = _dot(h, w_ref[:, off_b:])
        gb_ref[0] = conv[:, :conv_dim]
        gc = conv[:, conv_dim:2 * conv_dim]
        u = conv[:, 2 * conv_dim:]
        cu_ref[0] = gc * u


def _mix_in(x, mods, gain, w, qg, kg, ones, cos, sin, *, base, tm, latent):
    b, s, d = x.shape
    conv_dim = (w.shape[1] - ATTN_DIM - 2 * KV_DIM) // 3
    tok = lambda i, j: (i, j, 0)
    head = lambda i, j: (i, 0, j, 0)
    in_specs = [pl.BlockSpec((1, tm, d), tok),
                pl.BlockSpec((1, N_MOD, d), lambda i, j: (i, 0, 0)),
                _resident((1, d)),
                _resident(w.shape)]
    args = [x, mods, gain, w]
    if latent:
        in_specs.append(_resident((1, ATTN_DIM)))
        args.append(qg)
    in_specs += [_resident((1, LANES)), _resident(ones.shape)]
    args += [kg, ones]
    out_specs = [pl.BlockSpec((1, N_KV_HEADS, tm, LANES), head),
                 pl.BlockSpec((1, N_KV_HEADS, V_ROWS, tm), lambda i, j: (i, 0, 0, j))]
    out_shape = [jax.ShapeDtypeStruct((b, N_KV_HEADS, s, LANES), BF16),
                 jax.ShapeDtypeStruct((b, N_KV_HEADS, V_ROWS, s), BF16)]
    if latent:
        in_specs += [pl.BlockSpec((tm, LANES), lambda i, j: (j, 0))] * 2
        args += [cos, sin]
        out_specs = [pl.BlockSpec((1, 4, tm, LANES), head)] + out_specs + [
            pl.BlockSpec((1, tm, conv_dim), tok)] * 2
        out_shape = [jax.ShapeDtypeStruct((b, 4, s, LANES), BF16)] + out_shape + [
            jax.ShapeDtypeStruct((b, s, conv_dim), F32)] * 2
    return pl.pallas_call(
        functools.partial(_mix_in_kernel, base=base, latent=latent),
        grid=(b, s // tm),
        in_specs=in_specs,
        out_specs=out_specs,
        out_shape=out_shape,
        compiler_params=pltpu.CompilerParams(dimension_semantics=("arbitrary", "arbitrary"),
                                             vmem_limit_bytes=VMEM_LIMIT),
        name="mix_in_latent" if latent else "mix_in_context",
    )(*args)


def _attn_kernel(q_ref, k_ref, vt_ref, kc_ref, vct_ref, g_ref, o_ref, *, tk, stable):
    tq = q_ref.shape[2]
    n_chunks = k_ref.shape[2] // tk
    low2 = jax.lax.broadcasted_iota(jnp.int32, (2 * tq, LANES), 1) < HEAD_DIM

    heads = [None] * N_Q_HEADS
    for kv in range(N_KV_HEADS):
        q2 = q_ref[0, 2 * kv:2 * kv + 2].reshape(2 * tq, LANES).astype(F32)
        q4 = jnp.concatenate([jnp.where(low2, q2, 0.0), jnp.where(low2, 0.0, q2)], axis=0).astype(BF16)
        chunks = [(k_ref[0, kv, c * tk:(c + 1) * tk, :], vt_ref[0, kv, :, c * tk:(c + 1) * tk])
                  for c in range(n_chunks)]
        chunks.append((kc_ref[0, kv], vct_ref[0, kv]))
        m = acc = None
        for k_c, vt_c in chunks:
            s = _dot_nt(k_c, q4)
            if stable:
                m_c = jnp.max(s, axis=0, keepdims=True)
                m_new = m_c if m is None else jnp.maximum(m, m_c)
                pv = _dot(vt_c, jnp.exp2(s - m_new).astype(BF16))
                acc = pv if acc is None else acc * jnp.exp2(m - m_new) + pv
                m = m_new
            else:
                pv = _dot(vt_c, jnp.exp2(s).astype(BF16))
                acc = pv if acc is None else acc + pv
        o = acc[:HEAD_DIM] / acc[HEAD_DIM:HEAD_DIM + 1]
        for i, h in enumerate((4 * kv, 4 * kv + 2, 4 * kv + 1, 4 * kv + 3)):
            heads[h] = o[:, i * tq:(i + 1) * tq]
    a = jnp.concatenate(heads, axis=0)
    ms = jnp.mean(a * a, axis=0, keepdims=True)
    o_ref[0] = ((a * jax.lax.rsqrt(ms + EPS)).T * g_ref[...]).astype(BF16)


def _attention(q, k, vt, kc, vct, gain_t, *, tq, tk, stable):
    b, _, s, _ = q.shape
    lc = kc.shape[2]
    per_batch = lambda i, j: (i, 0, 0, 0)
    return pl.pallas_call(
        functools.partial(_attn_kernel, tk=tk, stable=stable),
        grid=(b, s // tq),
        in_specs=[pl.BlockSpec((1, 4, tq, LANES), lambda i, j: (i, 0, j, 0)),
                  pl.BlockSpec((1, N_KV_HEADS, s, LANES), per_batch),
                  pl.BlockSpec((1, N_KV_HEADS, V_ROWS, s), per_batch),
                  pl.BlockSpec((1, N_KV_HEADS, lc, LANES), per_batch),
                  pl.BlockSpec((1, N_KV_HEADS, V_ROWS, lc), per_batch),
                  _resident((1, ATTN_DIM))],
        out_specs=pl.BlockSpec((1, tq, ATTN_DIM), lambda i, j: (i, j, 0)),
        out_shape=jax.ShapeDtypeStruct((b, s, ATTN_DIM), BF16),
        compiler_params=pltpu.CompilerParams(dimension_semantics=("arbitrary", "arbitrary"),
                                             vmem_limit_bytes=VMEM_LIMIT),
        name="attention_stable" if stable else "attention",
    )(q, k, vt, kc, vct, gain_t)


def _mix_out_kernel(x_ref, an_ref, gb_ref, cu_ref, cup_ref, cun_ref, mod_ref, cw_ref, cg_ref,
                    wmix_ref, g2_ref, win_ref, wout_ref, fg_ref, o_ref, act_ref, *, n_sub):
    j = pl.program_id(1)
    tm = x_ref.shape[1]
    mod = mod_ref[0]
    cw = cw_ref[...]
    halo = cup_ref.shape[1]
    attn_dim = an_ref.shape[2]
    for rows in _sub_tiles(tm, n_sub):
        lo, hi = rows.start, rows.stop
        n = hi - lo
        cu = cu_ref[0, rows, :]
        if lo == 0:
            prev_row = jnp.where(j == 0, 0.0, cup_ref[0, halo - 1:halo, :])
        else:
            prev_row = cu_ref[0, lo - 1:lo, :]
        if hi == tm:
            next_row = jnp.where(j == pl.num_programs(1) - 1, 0.0, cun_ref[0, 0:1, :])
        else:
            next_row = cu_ref[0, hi:hi + 1, :]
        row = jax.lax.broadcasted_iota(jnp.int32, cu.shape, 0)
        cu_prev = jnp.where(row == 0, prev_row, pltpu.roll(cu, 1, 0))
        cu_next = jnp.where(row == n - 1, next_row, pltpu.roll(cu, n - 1, 0))
        sc = gb_ref[0, rows, :] * (cu_prev * cw[0:1, :] + cu * cw[1:2, :] + cu_next * cw[2:3, :])
        scn = _rms(sc, cg_ref[...]).astype(BF16)
        mix = _dot(an_ref[0, rows, :], wmix_ref[:attn_dim, :]) + _dot(scn, wmix_ref[attn_dim:, :])
        x = x_ref[0, rows, :] + mod[5:6, :] * mix
        h = _norm_mod(x, g2_ref[...], mod, 6).astype(BF16)
        y = _swiglu(h, win_ref, wout_ref, act_ref, rows)
        x = x + 0.5 * mod[8:9, :] * y
        o_ref[0, rows, :] = _rms(x, fg_ref[...])


def _mix_out(x, an, gb, cu, mods, conv_w, conv_gain, w_mix, gain2, w_in, w_out, final_gain, *, tm, n_sub,
             halo=8):
    b, s, d = x.shape
    d_ff = w_out.shape[0]
    conv_dim = cu.shape[2]
    tok = lambda i, j: (i, j, 0)
    per = tm // halo
    n_halo = s // halo
    return pl.pallas_call(
        functools.partial(_mix_out_kernel, n_sub=n_sub),
        grid=(b, s // tm),
        in_specs=[pl.BlockSpec((1, tm, d), tok),
                  pl.BlockSpec((1, tm, an.shape[2]), tok),
                  pl.BlockSpec((1, tm, conv_dim), tok),
                  pl.BlockSpec((1, tm, conv_dim), tok),
                  pl.BlockSpec((1, halo, conv_dim), lambda i, j: (i, jnp.maximum(j * per - 1, 0), 0)),
                  pl.BlockSpec((1, halo, conv_dim), lambda i, j: (i, jnp.minimum((j + 1) * per, n_halo - 1), 0)),
                  pl.BlockSpec((1, N_MOD, d), lambda i, j: (i, 0, 0)),
                  _resident(conv_w.shape),
                  _resident((1, conv_dim)),
                  _resident(w_mix.shape),
                  _resident((1, d)),
                  _resident(w_in.shape),
                  _resident(w_out.shape),
                  _resident((1, d))],
        out_specs=pl.BlockSpec((1, tm, d), tok),
        out_shape=jax.ShapeDtypeStruct(x.shape, F32),
        scratch_shapes=[pltpu.VMEM((tm, d_ff), BF16)],
        compiler_params=pltpu.CompilerParams(dimension_semantics=("arbitrary", "arbitrary"),
                                             vmem_limit_bytes=VMEM_LIMIT),
        name="mix_out_ffn2",
    )(x, an, gb, cu, cu, cu, mods, conv_w, conv_gain, w_mix, gain2, w_in, w_out, final_gain)


def _rope_tables(n_tokens):
    rows = n_tokens // GRID_W
    row = jnp.repeat(jnp.arange(rows, dtype=F32), GRID_W)
    col = jnp.tile(jnp.arange(GRID_W, dtype=F32), rows)
    n_freq = HEAD_DIM // 4
    inv = ROPE_THETA ** (-jnp.arange(n_freq, dtype=F32) / n_freq)
    ang = jnp.concatenate([row[:, None] * inv, col[:, None] * inv], axis=-1)
    cos = jnp.repeat(jnp.cos(ang), 2, axis=-1)
    sign = jnp.tile(jnp.array([-1.0, 1.0], F32), HEAD_DIM // 2)
    sin = jnp.repeat(jnp.sin(ang), 2, axis=-1) * sign
    return jnp.tile(cos, (1, LANES // HEAD_DIM)), jnp.tile(sin, (1, LANES // HEAD_DIM))


def kernel(x, c, ctx, c_ctx, w_ada, b_ada, norm_ffn1, w_ffn1_in, w_ffn1_out, norm_mix, w_mix_in,
           q_norm, k_norm, conv_w, attn_out_norm, conv_out_norm, w_mix_out, norm_ffn2, w_ffn2_in,
           w_ffn2_out, final_norm):
    b, s, d = x.shape
    lc = ctx.shape[1]
    assert w_ada.shape[0] == 1, "single-layer problem"
    row = lambda v: v.reshape(1, -1)

    n_rows = 16
    cc = jnp.concatenate([c, c_ctx[None], jnp.zeros((n_rows - b - 1, d), F32)], axis=0)
    mall = _ada_mod(cc, w_ada[0], row(b_ada[0]))
    mods = mall[:b].reshape(b, N_MOD, d)
    mods_c = jnp.broadcast_to(mall[b].reshape(1, N_MOD, d), (b, N_MOD, d))

    w1_in, w1_out = w_ffn1_in[0].astype(BF16), w_ffn1_out[0].astype(BF16)
    w2_in, w2_out = w_ffn2_in[0].astype(BF16), w_ffn2_out[0].astype(BF16)
    w_in, w_mo = w_mix_in[0].astype(BF16), w_mix_out[0].astype(BF16)

    x1 = _ffn(x, mods, row(norm_ffn1[0]), w1_in, w1_out, base=0, tm=1024, n_sub=2)
    c1 = _ffn(ctx, mods_c, row(norm_ffn1[0]), w1_in, w1_out, base=0, tm=lc, n_sub=1)

    cos, sin = _rope_tables(s)
    qg = jnp.tile(q_norm[0], N_Q_HEADS).reshape(1, ATTN_DIM) * (HEAD_DIM ** -0.5 * LOG2E)
    kg = jnp.tile(k_norm[0], N_KV_HEADS).reshape(1, KV_DIM)
    head_id = jnp.arange(2 * LANES) // HEAD_DIM
    ones = (head_id[:, None] == head_id[None, :]).astype(BF16)

    q, k, vt, gb, cu = _mix_in(x1, mods, row(norm_mix[0]), w_in, qg, kg, ones, cos, sin,
                              base=3, tm=512, latent=True)
    kc, vct = _mix_in(c1, mods_c, row(norm_mix[0]), w_in, None, kg, ones, None, None,
                     base=3, tm=lc, latent=False)

    score_bound = (HEAD_DIM ** 0.5 * LOG2E) * jnp.max(jnp.abs(q_norm[0])) * jnp.max(jnp.abs(k_norm[0]))
    attend = lambda stable: functools.partial(_attention, tq=256, tk=512, stable=stable)
    an = jax.lax.cond(score_bound <= MAX_EXP2_ARG, attend(False), attend(True),
                      q, k, vt, kc, vct, row(attn_out_norm[0]))

    return _mix_out(x1, an, gb, cu, mods, conv_w[0], row(conv_out_norm[0]), w_mo, row(norm_ffn2[0]),
                    w2_in, w2_out, row(final_norm), tm=1024, n_sub=2)
```

```python
import functools
import math

import jax
import jax.numpy as jnp
from jax.experimental import pallas as pl
from jax.experimental.pallas import tpu as pltpu

F32 = jnp.float32
BF16 = jnp.bfloat16

EPS = 1e-6
ROPE_THETA = 10000.0
GRID_W = 64
HEAD_DIM = 64
N_Q_HEADS = 8
N_KV_HEADS = 2
ATTN_DIM = N_Q_HEADS * HEAD_DIM
KV_DIM = N_KV_HEADS * HEAD_DIM
N_MOD = 9
LANES = 128
FF_CHUNK = 256
VMEM_LIMIT = 56 * 1024 * 1024
LOG2E = math.log2(math.e)
MAX_EXP2_ARG = 60.0
V_ROWS = 80


def _dot(a, b):
    return jnp.dot(a, b, preferred_element_type=F32)


def _dot_nt(a, b):
    return jax.lax.dot_general(a, b, (((1,), (1,)), ((), ())), preferred_element_type=F32)


def _rms(x, gain_row):
    ms = jnp.mean(x * x, axis=-1, keepdims=True)
    return x * jax.lax.rsqrt(ms + EPS) * gain_row


def _norm_mod(x, gain_row, mod, base):
    shift = mod[base:base + 1, :]
    scale = mod[base + 1:base + 2, :]
    return _rms(x, gain_row) * (1.0 + scale) + shift


def _swiglu(h, win_ref, wout_ref, act_ref, rows):
    d_ff = wout_ref.shape[0]
    for c in range(d_ff // FF_CHUNK):
        lo = c * FF_CHUNK
        g = _dot(h, win_ref[:, lo:lo + FF_CHUNK])
        u = _dot(h, win_ref[:, d_ff + lo:d_ff + lo + FF_CHUNK])
        act_ref[rows, lo:lo + FF_CHUNK] = (g * jax.nn.sigmoid(g) * u).astype(BF16)
    return _dot(act_ref[rows, :], wout_ref[...])


def _sub_tiles(n_rows, n_sub):
    sub = n_rows // n_sub
    return [slice(r * sub, (r + 1) * sub) for r in range(n_sub)]


def _ada_kernel(c_ref, w_ref, b_ref, o_ref):
    c = c_ref[...]
    h = (c * jax.nn.sigmoid(c)).astype(BF16)
    o_ref[...] = _dot(h, w_ref[...].astype(BF16)) + b_ref[...]


def _ada_mod(cc, w_ada, b_ada, tn=1024):
    rows, d = cc.shape
    n = w_ada.shape[1]
    return pl.pallas_call(
        _ada_kernel,
        grid=(n // tn,),
        in_specs=[pl.BlockSpec((rows, d), lambda j: (0, 0)),
                  pl.BlockSpec((d, tn), lambda j: (0, j)),
                  pl.BlockSpec((1, tn), lambda j: (0, j))],
        out_specs=pl.BlockSpec((rows, tn), lambda j: (0, j)),
        out_shape=jax.ShapeDtypeStruct((rows, n), F32),
        compiler_params=pltpu.CompilerParams(dimension_semantics=("arbitrary",),
                                             vmem_limit_bytes=VMEM_LIMIT),
        name="ada_mod",
    )(cc, w_ada, b_ada)


def _ffn_kernel(x_ref, mod_ref, g_ref, win_ref, wout_ref, o_ref, act_ref, *, base, n_sub):
    mod = mod_ref[0]
    for rows in _sub_tiles(x_ref.shape[1], n_sub):
        x = x_ref[0, rows, :]
        h = _norm_mod(x, g_ref[...], mod, base).astype(BF16)
        y = _swiglu(h, win_ref, wout_ref, act_ref, rows)
        o_ref[0, rows, :] = x + 0.5 * mod[base + 2:base + 3, :] * y


def _resident(shape):
    return pl.BlockSpec(shape, lambda *_: (0,) * len(shape), pipeline_mode=pl.Buffered(1))


def _ffn(x, mods, gain, w_in, w_out, *, base, tm, n_sub):
    b, s, d = x.shape
    d_ff = w_out.shape[0]
    return pl.pallas_call(
        functools.partial(_ffn_kernel, base=base, n_sub=n_sub),
        grid=(b, s // tm),
        in_specs=[pl.BlockSpec((1, tm, d), lambda i, j: (i, j, 0)),
                  pl.BlockSpec((1, N_MOD, d), lambda i, j: (i, 0, 0)),
                  _resident((1, d)),
                  _resident(w_in.shape),
                  _resident(w_out.shape)],
        out_specs=pl.BlockSpec((1, tm, d), lambda i, j: (i, j, 0)),
        out_shape=jax.ShapeDtypeStruct(x.shape, F32),
        scratch_shapes=[pltpu.VMEM((tm, d_ff), BF16)],
        compiler_params=pltpu.CompilerParams(dimension_semantics=("arbitrary", "arbitrary"),
                                             vmem_limit_bytes=VMEM_LIMIT),
        name="ffn",
    )(x, mods, gain, w_in, w_out)


def _pair_swap(x, even_lane):
    n = x.shape[-1]
    return jnp.where(even_lane, pltpu.roll(x, n - 1, 1), pltpu.roll(x, 1, 1))


def _head_sumsq(x, ones_ref):
    width = ones_ref.shape[0]
    ones = ones_ref[...]
    parts = []
    for j in range(x.shape[-1] // width):
        xj = x[:, j * width:(j + 1) * width]
        sq = xj * xj
        hi = sq.astype(BF16)
        lo = (sq - hi.astype(F32)).astype(BF16)
        parts.append(_dot(hi, ones) + _dot(lo, ones))
    return parts[0] if len(parts) == 1 else jnp.concatenate(parts, axis=-1)


def _qk_heads(p, ss, gain_row):
    return p * jax.lax.rsqrt(ss * (1.0 / HEAD_DIM) + EPS) * gain_row


def _mix_in_kernel(*refs, base, latent):
    if latent:
        (x_ref, mod_ref, g_ref, w_ref, qg_ref, kg_ref, ones_ref, cos_ref, sin_ref,
         q_ref, k_ref, v_ref, gb_ref, cu_ref) = refs
    else:
        x_ref, mod_ref, g_ref, w_ref, kg_ref, ones_ref, k_ref, v_ref = refs
    x = x_ref[0]
    h = _norm_mod(x, g_ref[...], mod_ref[0], base).astype(BF16)
    lane = jax.lax.broadcasted_iota(jnp.int32, (x.shape[0], LANES), 1)
    low_half = lane < HEAD_DIM

    off_k, off_v, off_b = ATTN_DIM, ATTN_DIM + KV_DIM, ATTN_DIM + 2 * KV_DIM
    conv_dim = (w_ref.shape[1] - off_b) // 3

    first = 0 if latent else off_k
    proj = _dot(h, w_ref[:, first:off_b])
    seg = lambda lo, hi: proj[:, lo - first:hi - first]
    sumsq = _head_sumsq(seg(first, off_b), ones_ref)
    seg_ss = lambda lo, hi: sumsq[:, lo - first:hi - first]
    k = _qk_heads(seg(off_k, off_v), seg_ss(off_k, off_v), kg_ref[...])
    if latent:
        even = (lane & 1) == 0
        cos = cos_ref[...]
        sin = sin_ref[...]
        k = k * cos + _pair_swap(k, even) * sin
    kr = pltpu.roll(k, HEAD_DIM, 1)
    k_ref[0, 0] = jnp.where(low_half, k, kr).astype(BF16)
    k_ref[0, 1] = jnp.where(low_half, kr, k).astype(BF16)

    v = seg(off_v, off_b)
    vr = pltpu.roll(v, HEAD_DIM, 1)
    ones_col = jnp.where(lane == HEAD_DIM, 1.0, 0.0)
    v_ref[0, 0] = jnp.where(low_half, v, ones_col).T[:V_ROWS].astype(BF16)
    v_ref[0, 1] = jnp.where(low_half, vr, ones_col).T[:V_ROWS].astype(BF16)

    if latent:
        q = _qk_heads(seg(0, off_k), seg_ss(0, off_k), qg_ref[...])
        for j in range(ATTN_DIM // LANES):
            qj = q[:, j * LANES:(j + 1) * LANES]
            q_ref[0, j] = (qj * cos + _pair_swap(qj, even) * sin).astype(BF16)
        conv = _dot(h, w_ref[:, off_b:])
        gb_ref[0] = conv[:, :conv_dim]
        gc = conv[:, conv_dim:2 * conv_dim]
        u = conv[:, 2 * conv_dim:]
        cu_ref[0] = gc * u


def _mix_in(x, mods, gain, w, qg, kg, ones, cos, sin, *, base, tm, latent):
    b, s, d = x.shape
    conv_dim = (w.shape[1] - ATTN_DIM - 2 * KV_DIM) // 3
    tok = lambda i, j: (i, j, 0)
    head = lambda i, j: (i, 0, j, 0)
    in_specs = [pl.BlockSpec((1, tm, d), tok),
                pl.BlockSpec((1, N_MOD, d), lambda i, j: (i, 0, 0)),
                _resident((1, d)),
                _resident(w.shape)]
    args = [x, mods, gain, w]
    if latent:
        in_specs.append(_resident((1, ATTN_DIM)))
        args.append(qg)
    in_specs += [_resident((1, LANES)), _resident(ones.shape)]
    args += [kg, ones]
    out_specs = [pl.BlockSpec((1, N_KV_HEADS, tm, LANES), head),
                 pl.BlockSpec((1, N_KV_HEADS, V_ROWS, tm), lambda i, j: (i, 0, 0, j))]
    out_shape = [jax.ShapeDtypeStruct((b, N_KV_HEADS, s, LANES), BF16),
                 jax.ShapeDtypeStruct((b, N_KV_HEADS, V_ROWS, s), BF16)]
    if latent:
        in_specs += [pl.BlockSpec((tm, LANES), lambda i, j: (j, 0))] * 2
        args += [cos, sin]
        out_specs = [pl.BlockSpec((1, 4, tm, LANES), head)] + out_specs + [
            pl.BlockSpec((1, tm, conv_dim), tok)] * 2
        out_shape = [jax.ShapeDtypeStruct((b, 4, s, LANES), BF16)] + out_shape + [
            jax.ShapeDtypeStruct((b, s, conv_dim), F32)] * 2
    return pl.pallas_call(
        functools.partial(_mix_in_kernel, base=base, latent=latent),
        grid=(b, s // tm),
        in_specs=in_specs,
        out_specs=out_specs,
        out_shape=out_shape,
        compiler_params=pltpu.CompilerParams(dimension_semantics=("arbitrary", "arbitrary"),
                                             vmem_limit_bytes=VMEM_LIMIT),
        name="mix_in_latent" if latent else "mix_in_context",
    )(*args)


def _attn_kernel(q_ref, k_ref, vt_ref, kc_ref, vct_ref, g_ref, o_ref, *, tk, stable):
    tq = q_ref.shape[2]
    n_chunks = k_ref.shape[2] // tk
    low2 = jax.lax.broadcasted_iota(jnp.int32, (2 * tq, LANES), 1) < HEAD_DIM

    heads = [None] * N_Q_HEADS
    for kv in range(N_KV_HEADS):
        q2 = q_ref[0, 2 * kv:2 * kv + 2].reshape(2 * tq, LANES).astype(F32)
        q4 = jnp.concatenate([jnp.where(low2, q2, 0.0), jnp.where(low2, 0.0, q2)], axis=0).astype(BF16)
        chunks = [(k_ref[0, kv, c * tk:(c + 1) * tk, :], vt_ref[0, kv, :, c * tk:(c + 1) * tk])
                  for c in range(n_chunks)]
        chunks.append((kc_ref[0, kv], vct_ref[0, kv]))
        m = acc = None
        for k_c, vt_c in chunks:
            s = _dot_nt(k_c, q4)
            if stable:
                m_c = jnp.max(s, axis=0, keepdims=True)
                m_new = m_c if m is None else jnp.maximum(m, m_c)
                pv = _dot(vt_c, jnp.exp2(s - m_new).astype(BF16))
                acc = pv if acc is None else acc * jnp.exp2(m - m_new) + pv
                m = m_new
            else:
                pv = _dot(vt_c, jnp.exp2(s).astype(BF16))
                acc = pv if acc is None else acc + pv
        o = acc[:HEAD_DIM] / acc[HEAD_DIM:HEAD_DIM + 1]
        for i, h in enumerate((4 * kv, 4 * kv + 2, 4 * kv + 1, 4 * kv + 3)):
            heads[h] = o[:, i * tq:(i + 1) * tq]
    a = jnp.concatenate(heads, axis=0)
    ms = jnp.mean(a * a, axis=0, keepdims=True)
    o_ref[0] = ((a * jax.lax.rsqrt(ms + EPS)).T * g_ref[...]).astype(BF16)


def _attention(q, k, vt, kc, vct, gain_t, *, tq, tk, stable):
    b, _, s, _ = q.shape
    lc = kc.shape[2]
    per_batch = lambda i, j: (i, 0, 0, 0)
    return pl.pallas_call(
        functools.partial(_attn_kernel, tk=tk, stable=stable),
        grid=(b, s // tq),
        in_specs=[pl.BlockSpec((1, 4, tq, LANES), lambda i, j: (i, 0, j, 0)),
                  pl.BlockSpec((1, N_KV_HEADS, s, LANES), per_batch),
                  pl.BlockSpec((1, N_KV_HEADS, V_ROWS, s), per_batch),
                  pl.BlockSpec((1, N_KV_HEADS, lc, LANES), per_batch),
                  pl.BlockSpec((1, N_KV_HEADS, V_ROWS, lc), per_batch),
                  _resident((1, ATTN_DIM))],
        out_specs=pl.BlockSpec((1, tq, ATTN_DIM), lambda i, j: (i, j, 0)),
        out_shape=jax.ShapeDtypeStruct((b, s, ATTN_DIM), BF16),
        compiler_params=pltpu.CompilerParams(dimension_semantics=("arbitrary", "arbitrary"),
                                             vmem_limit_bytes=VMEM_LIMIT),
        name="attention_stable" if stable else "attention",
    )(q, k, vt, kc, vct, gain_t)


def _mix_out_kernel(x_ref, an_ref, gb_ref, cu_ref, cup_ref, cun_ref, mod_ref, cw_ref, cg_ref,
                    wmix_ref, g2_ref, win_ref, wout_ref, fg_ref, o_ref, act_ref, *, n_sub):
    j = pl.program_id(1)
    tm = x_ref.shape[1]
    mod = mod_ref[0]
    cw = cw_ref[...]
    halo = cup_ref.shape[1]
    attn_dim = an_ref.shape[2]
    for rows in _sub_tiles(tm, n_sub):
        lo, hi = rows.start, rows.stop
        n = hi - lo
        cu = cu_ref[0, rows, :]
        if lo == 0:
            prev_row = jnp.where(j == 0, 0.0, cup_ref[0, halo - 1:halo, :])
        else:
            prev_row = cu_ref[0, lo - 1:lo, :]
        if hi == tm:
            next_row = jnp.where(j == pl.num_programs(1) - 1, 0.0, cun_ref[0, 0:1, :])
        else:
            next_row = cu_ref[0, hi:hi + 1, :]
        row = jax.lax.broadcasted_iota(jnp.int32, cu.shape, 0)
        cu_prev = jnp.where(row == 0, prev_row, pltpu.roll(cu, 1, 0))
        cu_next = jnp.where(row == n - 1, next_row, pltpu.roll(cu, n - 1, 0))
        sc = gb_ref[0, rows, :] * (cu_prev * cw[0:1, :] + cu * cw[1:2, :] + cu_next * cw[2:3, :])
        scn = _rms(sc, cg_ref[...]).astype(BF16)
        mix = _dot(an_ref[0, rows, :], wmix_ref[:attn_dim, :]) + _dot(scn, wmix_ref[attn_dim:, :])
        x = x_ref[0, rows, :] + mod[5:6, :] * mix
        h = _norm_mod(x, g2_ref[...], mod, 6).astype(BF16)
        y = _swiglu(h, win_ref, wout_ref, act_ref, rows)
        x = x + 0.5 * mod[8:9, :] * y
        o_ref[0, rows, :] = _rms(x, fg_ref[...])


def _mix_out(x, an, gb, cu, mods, conv_w, conv_gain, w_mix, gain2, w_in, w_out, final_gain, *, tm, n_sub,
             halo=8):
    b, s, d = x.shape
    d_ff = w_out.shape[0]
    conv_dim = cu.shape[2]
    tok = lambda i, j: (i, j, 0)
    per = tm // halo
    n_halo = s // halo
    return pl.pallas_call(
        functools.partial(_mix_out_kernel, n_sub=n_sub),
        grid=(b, s // tm),
        in_specs=[pl.BlockSpec((1, tm, d), tok),
                  pl.BlockSpec((1, tm, an.shape[2]), tok),
                  pl.BlockSpec((1, tm, conv_dim), tok),
                  pl.BlockSpec((1, tm, conv_dim), tok),
                  pl.BlockSpec((1, halo, conv_dim), lambda i, j: (i, jnp.maximum(j * per - 1, 0), 0)),
                  pl.BlockSpec((1, halo, conv_dim), lambda i, j: (i, jnp.minimum((j + 1) * per, n_halo - 1), 0)),
                  pl.BlockSpec((1, N_MOD, d), lambda i, j: (i, 0, 0)),
                  _resident(conv_w.shape),
                  _resident((1, conv_dim)),
                  _resident(w_mix.shape),
                  _resident((1, d)),
                  _resident(w_in.shape),
                  _resident(w_out.shape),
                  _resident((1, d))],
        out_specs=pl.BlockSpec((1, tm, d), tok),
        out_shape=jax.ShapeDtypeStruct(x.shape, F32),
        scratch_shapes=[pltpu.VMEM((tm, d_ff), BF16)],
        compiler_params=pltpu.CompilerParams(dimension_semantics=("arbitrary", "arbitrary"),
                                             vmem_limit_bytes=VMEM_LIMIT),
        name="mix_out_ffn2",
    )(x, an, gb, cu, cu, cu, mods, conv_w, conv_gain, w_mix, gain2, w_in, w_out, final_gain)


def _rope_tables(n_tokens):
    rows = n_tokens // GRID_W
    row = jnp.repeat(jnp.arange(rows, dtype=F32), GRID_W)
    col = jnp.tile(jnp.arange(GRID_W, dtype=F32), rows)
    n_freq = HEAD_DIM // 4
    inv = ROPE_THETA ** (-jnp.arange(n_freq, dtype=F32) / n_freq)
    ang = jnp.concatenate([row[:, None] * inv, col[:, None] * inv], axis=-1)
    cos = jnp.repeat(jnp.cos(ang), 2, axis=-1)
    sign = jnp.tile(jnp.array([-1.0, 1.0], F32), HEAD_DIM // 2)
    sin = jnp.repeat(jnp.sin(ang), 2, axis=-1) * sign
    return jnp.tile(cos, (1, LANES // HEAD_DIM)), jnp.tile(sin, (1, LANES // HEAD_DIM))


def kernel(x, c, ctx, c_ctx, w_ada, b_ada, norm_ffn1, w_ffn1_in, w_ffn1_out, norm_mix, w_mix_in,
           q_norm, k_norm, conv_w, attn_out_norm, conv_out_norm, w_mix_out, norm_ffn2, w_ffn2_in,
           w_ffn2_out, final_norm):
    b, s, d = x.shape
    lc = ctx.shape[1]
    assert w_ada.shape[0] == 1, "single-layer problem"
    row = lambda v: v.reshape(1, -1)

    n_rows = 16
    cc = jnp.concatenate([c, c_ctx[None], jnp.zeros((n_rows - b - 1, d), F32)], axis=0)
    mall = _ada_mod(cc, w_ada[0], row(b_ada[0]))
    mods = mall[:b].reshape(b, N_MOD, d)
    mods_c = jnp.broadcast_to(mall[b].reshape(1, N_MOD, d), (b, N_MOD, d))

    w1_in, w1_out = w_ffn1_in[0].astype(BF16), w_ffn1_out[0].astype(BF16)
    w2_in, w2_out = w_ffn2_in[0].astype(BF16), w_ffn2_out[0].astype(BF16)
    w_in, w_mo = w_mix_in[0].astype(BF16), w_mix_out[0].astype(BF16)

    x1 = _ffn(x, mods, row(norm_ffn1[0]), w1_in, w1_out, base=0, tm=1024, n_sub=2)
    c1 = _ffn(ctx, mods_c, row(norm_ffn1[0]), w1_in, w1_out, base=0, tm=lc, n_sub=1)

    cos, sin = _rope_tables(s)
    qg = jnp.tile(q_norm[0], N_Q_HEADS).reshape(1, ATTN_DIM) * (HEAD_DIM ** -0.5 * LOG2E)
    kg = jnp.tile(k_norm[0], N_KV_HEADS).reshape(1, KV_DIM)
    head_id = jnp.arange(2 * LANES) // HEAD_DIM
    ones = (head_id[:, None] == head_id[None, :]).astype(BF16)

    q, k, vt, gb, cu = _mix_in(x1, mods, row(norm_mix[0]), w_in, qg, kg, ones, cos, sin,
                              base=3, tm=512, latent=True)
    kc, vct = _mix_in(c1, mods_c, row(norm_mix[0]), w_in, None, kg, ones, None, None,
                     base=3, tm=lc, latent=False)

    score_bound = (HEAD_DIM ** 0.5 * LOG2E) * jnp.max(jnp.abs(q_norm[0])) * jnp.max(jnp.abs(k_norm[0]))
    attend = lambda stable: functools.partial(_attention, tq=256, tk=2048, stable=stable)
    an = jax.lax.cond(score_bound <= MAX_EXP2_ARG, attend(False), attend(True),
                      q, k, vt, kc, vct, row(attn_out_norm[0]))

    return _mix_out(x1, an, gb, cu, mods, conv_w[0], row(conv_out_norm[0]), w_mo, row(norm_ffn2[0]),
                    w2_in, w2_out, row(final_norm), tm=1024, n_sub=2)
```
